```python
import math
import jax, jax.numpy as jnp
from jax import lax
import numpy as np

D_MODEL = 1024
BATCH = 16
SEQ = 2048
DEPTH = 1
DEC_BATCH = 128
DEC_SEQ = 8
PAST_LEN = 16384
PAGE_SIZE = 128

N_HEADS = 8
QK_NOPE = 64
QK_ROPE = 32
QK_HEAD = QK_NOPE + QK_ROPE
V_HEAD = 64
Q_LORA = 384
KV_LORA = 256
ROPE_THETA = 10000.0
ATTN_SCALE = QK_HEAD ** -0.5
Q_BLOCK = 128
CONV_DIM = 512
CONV_WIDTH = 3
D_FF = int(math.ceil(8 * D_MODEL / 3 / 256)) * 256
ALPHA = (2.0 * DEPTH) ** 0.25
BETA = (8.0 * DEPTH) ** -0.25
LN_EPS = 1e-5
RMS_EPS = 1e-6
SPLITS = [Q_LORA, Q_LORA + KV_LORA, Q_LORA + KV_LORA + QK_ROPE, Q_LORA + KV_LORA + QK_ROPE + 3 * CONV_DIM]
N_IN = SPLITS[-1] + 2 * D_MODEL

kernel_name = 'hybrid_mla_shortconv_adaln_deepnorm_step'


def layer_norm(x, g, b):
    xf = x.astype(jnp.float32)
    mu = jnp.mean(xf, -1, keepdims=True)
    var = jnp.mean(jnp.square(xf - mu), -1, keepdims=True)
    return ((xf - mu) * lax.rsqrt(var + LN_EPS) * g + b).astype(x.dtype)


def rms_norm(x, g):
    xf = x.astype(jnp.float32)
    return (xf * lax.rsqrt(jnp.mean(xf * xf, -1, keepdims=True) + RMS_EPS) * g).astype(x.dtype)


def rope_tables(pos):
    inv = ROPE_THETA ** (-jnp.arange(0, QK_ROPE, 2, dtype=jnp.float32) / QK_ROPE)
    ang = pos.astype(jnp.float32)[:, None] * inv[None, :]
    return jnp.cos(ang), jnp.sin(ang)


def apply_rope(x, cos, sin):
    x1, x2 = jnp.split(x.astype(jnp.float32), 2, axis=-1)
    return jnp.concatenate([x1 * cos - x2 * sin, x1 * sin + x2 * cos], -1).astype(x.dtype)


def ada_terms(c, w_ada, b_ada):
    a = jax.nn.silu(c) @ w_ada + b_ada
    return jnp.split(a[:, None, :], 6, axis=-1)


def mixer_in(h, cos, sin, w_in, q_norm_g, kv_norm_g, w_uq):
    B, T = h.shape[0], h.shape[1]
    z = h @ w_in
    q_lat, kv_lat, kr, conv_in, gates = jnp.split(z, SPLITS, axis=-1)
    q = (rms_norm(q_lat, q_norm_g) @ w_uq).reshape(B, T, N_HEADS, QK_HEAD)
    q_nope = q[..., :QK_NOPE]
    q_rope = apply_rope(q[..., QK_NOPE:], cos[:, None, :], sin[:, None, :])
    c_kv = rms_norm(kv_lat, kv_norm_g)
    k_rope = apply_rope(kr, cos, sin)
    b_g, c_g, v_c = jnp.split(conv_in, 3, axis=-1)
    u = c_g * v_c
    g_a, g_c = jnp.split(gates, 2, axis=-1)
    return q_nope, q_rope, c_kv, k_rope, u, b_g, g_a, g_c


def short_conv(full, conv_w, T):
    out = full[:, 0:T] * conv_w[0]
    for k in range(1, CONV_WIDTH):
        out = out + full[:, k:k + T] * conv_w[k]
    return out


def mixer_out(o, conv_y, b_g, g_a, g_c, w_oa, w_oc, w_o):
    B, T = o.shape[0], o.shape[1]
    y_a = o.reshape(B, T, N_HEADS * V_HEAD) @ w_oa
    y_c = (b_g * conv_y) @ w_oc
    m = jax.nn.sigmoid(g_a) * y_a + jax.nn.sigmoid(g_c) * y_c
    return m @ w_o


def finish(x, mix, ada, ln1_g, ln1_b, w_ff1, w_ff3, w_ff2, ln2_g, ln2_b):
    _, _, g1, sh2, sc2, g2 = ada
    x = layer_norm(ALPHA * x + g1 * mix, ln1_g, ln1_b)
    h2 = x * (1 + sc2) + sh2
    f = (jax.nn.silu(h2 @ w_ff1) * (h2 @ w_ff3)) @ w_ff2
    return layer_norm(ALPHA * x + g2 * f, ln2_g, ln2_b)


def prompt_attention(q_nope, q_rope, c_kv, k_rope, w_ukv):
    B, S = c_kv.shape[0], c_kv.shape[1]
    kv = jnp.einsum('bsl,lhd->bshd', c_kv, w_ukv)
    k_nope, v = kv[..., :QK_NOPE], kv[..., QK_NOPE:]
    k = jnp.concatenate([k_nope, jnp.broadcast_to(k_rope[:, :, None, :], (B, S, N_HEADS, QK_ROPE))], -1)
    q = jnp.concatenate([q_nope, q_rope], -1)
    nb = S // Q_BLOCK
    qb = q.reshape(B, nb, Q_BLOCK, N_HEADS, QK_HEAD).transpose(1, 0, 2, 3, 4)
    kpos = jnp.arange(S)

    def block(args):
        qi, i = args
        s = jnp.einsum('bqhd,bkhd->bhqk', qi, k, preferred_element_type=jnp.float32) * ATTN_SCALE
        qpos = i * Q_BLOCK + jnp.arange(Q_BLOCK)
        s = jnp.where(kpos[None, :] <= qpos[:, None], s, -jnp.inf)
        p = jax.nn.softmax(s, axis=-1).astype(v.dtype)
        return jnp.einsum('bhqk,bkhv->bqhv', p, v)

    o = lax.map(block, (qb, jnp.arange(nb)))
    return o.transpose(1, 0, 2, 3, 4).reshape(B, S, N_HEADS, V_HEAD)


def sample_attention(q_nope, q_rope, c_new, kr_new, cache_kv_latent, cache_k_rope, layer, page_table, w_ukv):
    f32 = jnp.float32
    w_uk, w_uv = w_ukv[..., :QK_NOPE], w_ukv[..., QK_NOPE:]
    q_abs = jnp.einsum('bthn,lhn->bhtl', q_nope, w_uk, preferred_element_type=f32)
    q_r = q_rope.transpose(0, 2, 1, 3).astype(f32)

    def scores(c, kr):
        return (jnp.einsum('bhtl,bkl->bhtk', q_abs, c.astype(f32))
                + jnp.einsum('bhtr,bkr->bhtk', q_r, kr.astype(f32))) * ATTN_SCALE

    T = c_new.shape[1]
    s = jnp.where(jnp.tril(jnp.ones((T, T), bool)), scores(c_new, kr_new), -jnp.inf)
    m = jnp.max(s, -1)
    p = jnp.exp(s - m[..., None])
    l = jnp.sum(p, -1)
    acc = jnp.einsum('bhtk,bkl->bhtl', p, c_new.astype(f32))

    def step(carry, pid):
        m, l, acc = carry
        c = cache_kv_latent[layer, pid]
        kr = cache_k_rope[layer, pid]
        s = scores(c, kr)
        m_new = jnp.maximum(m, jnp.max(s, -1))
        corr = jnp.exp(m - m_new)
        p = jnp.exp(s - m_new[..., None])
        acc = acc * corr[..., None] + jnp.einsum('bhtk,bkl->bhtl', p, c.astype(f32))
        return (m_new, l * corr + jnp.sum(p, -1), acc), None

    (m, l, acc), _ = lax.scan(step, (m, l, acc), page_table.T)
    o_lat = acc / l[..., None]
    return jnp.einsum('bhtl,lhv->bthv', o_lat, w_uv.astype(f32)).astype(c_new.dtype)


def setup_inputs(seed: int = 0) -> dict:
    key = jax.random.key(seed)
    ks = jax.random.split(key, 32)
    n_pages = PAST_LEN // PAGE_SIZE
    n_pool = -(-5 * DEC_BATCH * n_pages // 4)
    L = DEPTH

    def nrm(k, shape, scale):
        return jax.random.normal(k, shape, jnp.float32) * scale

    page_table = jax.random.permutation(ks[5], n_pool)[:DEC_BATCH * n_pages].reshape(DEC_BATCH, n_pages).astype(jnp.int32)
    return {
        'x_prompt': nrm(ks[0], (BATCH, SEQ, D_MODEL), 1.0),
        'x_sample': nrm(ks[1], (DEC_BATCH, DEC_SEQ, D_MODEL), 1.0),
        'cache_kv_latent': nrm(ks[2], (L, n_pool, PAGE_SIZE, KV_LORA), 1.0),
        'cache_k_rope': nrm(ks[3], (L, n_pool, PAGE_SIZE, QK_ROPE), 1.0),
        'state_conv': nrm(ks[4], (L, DEC_BATCH, CONV_WIDTH - 1, CONV_DIM), 1.0),
        'page_table': page_table,
        'c_prompt': nrm(ks[6], (BATCH, D_MODEL), 1.0),
        'c_sample': nrm(ks[7], (DEC_BATCH, D_MODEL), 1.0),
        'w_ada': nrm(ks[8], (L, D_MODEL, 6 * D_MODEL), 0.5 * D_MODEL ** -0.5),
        'b_ada': nrm(ks[9], (L, 6 * D_MODEL), 0.01),
        'w_in': nrm(ks[10], (L, D_MODEL, N_IN), D_MODEL ** -0.5),
        'q_norm_g': 1.0 + nrm(ks[11], (L, Q_LORA), 0.01),
        'kv_norm_g': 1.0 + nrm(ks[12], (L, KV_LORA), 0.01),
        'w_uq': nrm(ks[13], (L, Q_LORA, N_HEADS * QK_HEAD), Q_LORA ** -0.5),
        'w_ukv': nrm(ks[14], (L, KV_LORA, N_HEADS, QK_NOPE + V_HEAD), KV_LORA ** -0.5),
        'w_oa': nrm(ks[15], (L, N_HEADS * V_HEAD, D_MODEL), BETA * (N_HEADS * V_HEAD) ** -0.5),
        'conv_w': nrm(ks[16], (L, CONV_WIDTH, CONV_DIM), CONV_WIDTH ** -0.5),
        'w_oc': nrm(ks[17], (L, CONV_DIM, D_MODEL), BETA * CONV_DIM ** -0.5),
        'w_o': nrm(ks[18], (L, D_MODEL, D_MODEL), BETA * D_MODEL ** -0.5),
        'ln1_g': 1.0 + nrm(ks[19], (L, D_MODEL), 0.01),
        'ln1_b': nrm(ks[20], (L, D_MODEL), 0.01),
        'w_ff1': nrm(ks[21], (L, D_MODEL, D_FF), D_MODEL ** -0.5),
        'w_ff3': nrm(ks[22], (L, D_MODEL, D_FF), D_MODEL ** -0.5),
        'w_ff2': nrm(ks[23], (L, D_FF, D_MODEL), BETA * D_FF ** -0.5),
        'ln2_g': 1.0 + nrm(ks[24], (L, D_MODEL), 0.01),
        'ln2_b': nrm(ks[25], (L, D_MODEL), 0.01),
    }


def reference(x_prompt, x_sample, cache_kv_latent, cache_k_rope, state_conv, page_table, c_prompt, c_sample,
              w_ada, b_ada, w_in, q_norm_g, kv_norm_g, w_uq, w_ukv, w_oa, conv_w, w_oc, w_o,
              ln1_g, ln1_b, w_ff1, w_ff3, w_ff2, ln2_g, ln2_b):
    S = x_prompt.shape[1]
    T = x_sample.shape[1]
    cos_p, sin_p = rope_tables(jnp.arange(S))
    cos_s, sin_s = rope_tables(PAST_LEN + jnp.arange(T))
    xp, xs = x_prompt, x_sample
    lat_p, kr_p, conv_p, lat_s, kr_s, conv_s = [], [], [], [], [], []
    for l in range(DEPTH):
        ada = ada_terms(c_prompt, w_ada[l], b_ada[l])
        h = xp * (1 + ada[1]) + ada[0]
        qn, qr, ckv, kr, u, bg, ga, gc = mixer_in(h, cos_p, sin_p, w_in[l], q_norm_g[l], kv_norm_g[l], w_uq[l])
        o = prompt_attention(qn, qr, ckv, kr, w_ukv[l])
        full = jnp.pad(u, ((0, 0), (CONV_WIDTH - 1, 0), (0, 0)))
        cy = short_conv(full, conv_w[l], S)
        mix = mixer_out(o, cy, bg, ga, gc, w_oa[l], w_oc[l], w_o[l])
        xp = finish(xp, mix, ada, ln1_g[l], ln1_b[l], w_ff1[l], w_ff3[l], w_ff2[l], ln2_g[l], ln2_b[l])
        lat_p.append(ckv)
        kr_p.append(kr)
        conv_p.append(full[:, full.shape[1] - (CONV_WIDTH - 1):])
        ada = ada_terms(c_sample, w_ada[l], b_ada[l])
        h = xs * (1 + ada[1]) + ada[0]
        qn, qr, ckv, kr, u, bg, ga, gc = mixer_in(h, cos_s, sin_s, w_in[l], q_norm_g[l], kv_norm_g[l], w_uq[l])
        o = sample_attention(qn, qr, ckv, kr, cache_kv_latent, cache_k_rope, l, page_table, w_ukv[l])
        full = jnp.concatenate([state_conv[l].astype(u.dtype), u], axis=1)
        cy = short_conv(full, conv_w[l], T)
        mix = mixer_out(o, cy, bg, ga, gc, w_oa[l], w_oc[l], w_o[l])
        xs = finish(xs, mix, ada, ln1_g[l], ln1_b[l], w_ff1[l], w_ff3[l], w_ff2[l], ln2_g[l], ln2_b[l])
        lat_s.append(ckv)
        kr_s.append(kr)
        conv_s.append(full[:, full.shape[1] - (CONV_WIDTH - 1):])
    return (xp, xs, jnp.stack(lat_p), jnp.stack(kr_p), jnp.stack(conv_p), jnp.stack(lat_s), jnp.stack(kr_s), jnp.stack(conv_s))
```

```python
import functools
import math

import jax
import jax.numpy as jnp
from jax import lax
from jax.experimental import pallas as pl
from jax.experimental.pallas import tpu as pltpu

D_MODEL = 1024
N_HEADS = 8
QK_NOPE = 64
QK_ROPE = 32
QK_HEAD = QK_NOPE + QK_ROPE
V_HEAD = 64
Q_LORA = 384
KV_LORA = 256
ROPE_THETA = 10000.0
ATTN_SCALE = QK_HEAD ** -0.5
CONV_DIM = 512
CONV_WIDTH = 3
D_FF = int(math.ceil(8 * D_MODEL / 3 / 256)) * 256
DEPTH = 1
ALPHA = (2.0 * DEPTH) ** 0.25
LN_EPS = 1e-5
RMS_EPS = 1e-6

LANES = 128
SUBLANES = 8
HEAD_SLAB = LANES
ROPE_LO = QK_NOPE
ROPE_HALF = QK_ROPE // 2
VMEM_LIMIT = 52 * 1024 * 1024

BF16 = jnp.bfloat16
F32 = jnp.float32
NT_DIMS = (((1,), (1,)), ((), ()))


def _params(n_axes=1, vmem=VMEM_LIMIT):
    return pltpu.CompilerParams(dimension_semantics=("arbitrary",) * n_axes, vmem_limit_bytes=vmem)


def _const_spec(shape):
    nd = len(shape)
    return pl.BlockSpec(shape, lambda *_: (0,) * nd, pipeline_mode=pl.Buffered(1))


def _mm(a, b):
    return jnp.dot(a, b, preferred_element_type=F32)


def _sigmoid(x):
    return 1.0 / (1.0 + jnp.exp(-x))


def _layer_norm(x, g, b):
    mu = jnp.mean(x, axis=-1, keepdims=True)
    xc = x - mu
    var = jnp.mean(xc * xc, axis=-1, keepdims=True)
    return xc * lax.rsqrt(var + LN_EPS) * g + b


def _rms_norm(x, g):
    return x * lax.rsqrt(jnp.mean(x * x, axis=-1, keepdims=True) + RMS_EPS) * g


def _rope_slab(x, cos_t, sin_t):
    lane = lax.broadcasted_iota(jnp.int32, x.shape, 1)
    swapped = jnp.where(lane < ROPE_LO + ROPE_HALF,
                        pltpu.roll(x, LANES - ROPE_HALF, 1), pltpu.roll(x, ROPE_HALF, 1))
    return x * cos_t + swapped * sin_t


def _ada_kernel(c_ref, w_ref, b_ref, o_ref):
    c = c_ref[...]
    s = c * _sigmoid(c)
    o_ref[...] = _mm(s.astype(BF16), w_ref[...]) + b_ref[...]


def _ada_terms(c, w_ada, b_ada):
    n, tn = c.shape[0], 1536
    return pl.pallas_call(
        _ada_kernel,
        grid=(6 * D_MODEL // tn,),
        in_specs=[pl.BlockSpec((n, D_MODEL), lambda j: (0, 0)),
                  pl.BlockSpec((D_MODEL, tn), lambda j: (0, j)),
                  pl.BlockSpec((1, tn), lambda j: (0, j))],
        out_specs=pl.BlockSpec((n, tn), lambda j: (0, j)),
        out_shape=jax.ShapeDtypeStruct((n, 6 * D_MODEL), F32),
        compiler_params=_params(),
        name="ada_terms",
    )(c, w_ada, b_ada)


def _mixer_in_kernel(*refs, prompt, tm, tiles_per_seq):
    if prompt:
        (x_ref, sh_ref, sc_ref, wa_ref, wb_ref, qg_ref, kvg_ref, wuq_ref, convw_ref,
         cq_ref, sq_ref, ck_ref, sk_ref, wuk_ref, wuv_ref,
         ckv_ref, kr_ref, cb_ref, u_ref, q_ref, k_ref, v_ref, ubuf) = refs
    else:
        (x_ref, sh_ref, sc_ref, wa_ref, wb_ref, qg_ref, kvg_ref, wuq_ref, convw_ref,
         cq_ref, sq_ref, ck_ref, sk_ref, wukt_ref, fix1_ref, fix2_ref,
         ckv_ref, kr_ref, cb_ref, u_ref, qabs_ref, qr_ref, ubuf) = refs
    i = pl.program_id(0)

    h = (x_ref[...] * (1.0 + sc_ref[...]) + sh_ref[...]).astype(BF16)

    za = _mm(h, wa_ref[...])
    q_lat = za[:, :Q_LORA]
    kv_lat = za[:, Q_LORA:Q_LORA + KV_LORA]
    kr_slab = za[:, Q_LORA + KV_LORA:]
    c_kv = _rms_norm(kv_lat, kvg_ref[...])
    ckv_ref[...] = c_kv
    kr_rot = _rope_slab(kr_slab, ck_ref[...], sk_ref[...])
    kr_ref[...] = kr_rot[:, ROPE_LO:ROPE_LO + QK_ROPE]

    q_all = _mm(_rms_norm(q_lat, qg_ref[...]).astype(BF16), wuq_ref[...])
    cq, sq = cq_ref[...], sq_ref[...]
    q_heads = [_rope_slab(q_all[:, hd * HEAD_SLAB:(hd + 1) * HEAD_SLAB], cq, sq) for hd in range(N_HEADS)]

    zb = _mm(h, wb_ref[...])
    b_g = zb[:, :CONV_DIM]
    u = zb[:, CONV_DIM:2 * CONV_DIM] * zb[:, 2 * CONV_DIM:]

    @pl.when(i % tiles_per_seq == 0)
    def _():
        ubuf[0:SUBLANES, :] = jnp.zeros((SUBLANES, CONV_DIM), F32)

    ubuf[SUBLANES:SUBLANES + tm, :] = u
    u1 = ubuf[pl.ds(SUBLANES - 1, tm), :]
    u2 = ubuf[pl.ds(SUBLANES - 2, tm), :]
    if prompt:
        ubuf[0:SUBLANES, :] = u[tm - SUBLANES:, :]
        u_ref[...] = u[tm - SUBLANES:, :]
    else:
        t = lax.broadcasted_iota(jnp.int32, (tm, CONV_DIM), 0) % SUBLANES
        u1 = jnp.where(t == 0, fix1_ref[...], u1)
        u2 = jnp.where(t < 2, fix2_ref[...], u2)
        u_ref[...] = u
    cw = convw_ref[...]
    conv_y = u2 * cw[0:1, :] + u1 * cw[1:2, :] + u * cw[2:3, :]
    cb_ref[...] = (b_g * conv_y).astype(BF16)

    if prompt:
        c_bf = c_kv.astype(BF16)
        k_nope = _mm(c_bf, wuk_ref[...])
        v_ref[...] = _mm(c_bf, wuv_ref[...]).astype(BF16)
        for hd in range(N_HEADS):
            q_ref[hd] = q_heads[hd].astype(BF16)
            k_ref[hd] = (k_nope[:, hd * HEAD_SLAB:(hd + 1) * HEAD_SLAB] + kr_rot).astype(BF16)
    else:
        nseq = tm // SUBLANES
        for hd in range(N_HEADS):
            qh = q_heads[hd]
            qabs = _mm(qh.astype(BF16), wukt_ref[hd])
            qabs_ref[:, hd] = qabs.reshape(nseq, SUBLANES, KV_LORA)
            qr_ref[:, hd] = qh[:, ROPE_LO:ROPE_LO + QK_ROPE].reshape(nseq, SUBLANES, QK_ROPE)


def _mixer_in(x2d, sh, sc, wts, tabs, *, prompt, nb, seq, tm, extra):
    ntok = nb * seq
    tiles_per_seq = seq // tm if prompt else 1
    grid = (ntok // tm,)
    row = lambda w: pl.BlockSpec((tm, w), lambda i: (i, 0))
    if prompt:
        ada_spec = pl.BlockSpec((None, 1, D_MODEL), lambda i: (i // tiles_per_seq, 0, 0))
        tab_spec = pl.BlockSpec((tm, LANES), lambda i: (i % tiles_per_seq, 0))
    else:
        ada_spec = row(D_MODEL)
        tab_spec = pl.BlockSpec((tm, LANES), lambda i: (0, 0))
    wa, wb, qg, kvg, wuq, convw = wts
    in_specs = [row(D_MODEL), ada_spec, ada_spec,
                _const_spec(wa.shape), _const_spec(wb.shape), _const_spec(qg.shape), _const_spec(kvg.shape),
                _const_spec(wuq.shape), _const_spec(convw.shape),
                tab_spec, tab_spec, tab_spec, tab_spec]
    out_shape = [jax.ShapeDtypeStruct((ntok, KV_LORA), F32), jax.ShapeDtypeStruct((ntok, QK_ROPE), F32),
                 jax.ShapeDtypeStruct((ntok, CONV_DIM), BF16)]
    out_specs = [row(KV_LORA), row(QK_ROPE), row(CONV_DIM)]
    if prompt:
        wuk, wuv = extra
        in_specs += [_const_spec(wuk.shape), _const_spec(wuv.shape)]
        head_spec = pl.BlockSpec((None, N_HEADS, tm, HEAD_SLAB),
                                 lambda i: (i // tiles_per_seq, 0, i % tiles_per_seq, 0))
        out_shape += [jax.ShapeDtypeStruct((nb * SUBLANES, CONV_DIM), F32),
                      jax.ShapeDtypeStruct((nb, N_HEADS, seq, HEAD_SLAB), BF16),
                      jax.ShapeDtypeStruct((nb, N_HEADS, seq, HEAD_SLAB), BF16),
                      jax.ShapeDtypeStruct((ntok, N_HEADS * V_HEAD), BF16)]
        out_specs += [pl.BlockSpec((SUBLANES, CONV_DIM), lambda i: (i // tiles_per_seq, 0)),
                      head_spec, head_spec, row(N_HEADS * V_HEAD)]
    else:
        wukt, fix1, fix2 = extra
        in_specs += [_const_spec(wukt.shape), row(CONV_DIM), row(CONV_DIM)]
        nseq = tm // SUBLANES
        out_shape += [jax.ShapeDtypeStruct((ntok, CONV_DIM), F32),
                      jax.ShapeDtypeStruct((nb, N_HEADS, SUBLANES, KV_LORA), F32),
                      jax.ShapeDtypeStruct((nb, N_HEADS, SUBLANES, QK_ROPE), F32)]
        out_specs += [row(CONV_DIM),
                      pl.BlockSpec((nseq, N_HEADS, SUBLANES, KV_LORA), lambda i: (i, 0, 0, 0)),
                      pl.BlockSpec((nseq, N_HEADS, SUBLANES, QK_ROPE), lambda i: (i, 0, 0, 0))]
    return pl.pallas_call(
        functools.partial(_mixer_in_kernel, prompt=prompt, tm=tm, tiles_per_seq=tiles_per_seq),
        grid=grid, in_specs=in_specs, out_specs=out_specs, out_shape=out_shape,
        scratch_shapes=[pltpu.VMEM((SUBLANES + tm, CONV_DIM), F32)],
        compiler_params=_params(),
        name="mixer_in_prompt" if prompt else "mixer_in_sample",
    )(x2d, sh, sc, wa, wb, qg, kvg, wuq, convw, *tabs, *extra)


def _prompt_attn_kernel(q_ref, k_ref, v_ref, o_ref, *, seq, tq):
    lane = lax.broadcasted_iota(jnp.int32, (tq, LANES), 1)
    tri = (lax.broadcasted_iota(jnp.int32, (tq, tq), 1) <= lax.broadcasted_iota(jnp.int32, (tq, tq), 0))
    for i in range(seq // tq):
        lo, hi = i * tq, (i + 1) * tq
        outs = []
        for hh in range(2):
            q = q_ref[hh, lo:hi, :]
            sd = lax.dot_general(q, k_ref[hh, lo:hi, :], NT_DIMS, preferred_element_type=F32)
            sd = jnp.where(tri, sd, -jnp.inf)
            m = jnp.max(sd, axis=-1, keepdims=True)
            if i > 0:
                so = lax.dot_general(q, k_ref[hh, 0:lo, :], NT_DIMS, preferred_element_type=F32)
                m = jnp.maximum(m, jnp.max(so, axis=-1, keepdims=True))
                po = jnp.exp(so - m)
                l = jnp.sum(po, axis=-1, keepdims=True)
                acc = _mm(po.astype(BF16), v_ref[0:lo, :])
            pd = jnp.exp(sd - m)
            ld = jnp.sum(pd, axis=-1, keepdims=True)
            accd = _mm(pd.astype(BF16), v_ref[lo:hi, :])
            if i > 0:
                l, acc = l + ld, acc + accd
            else:
                l, acc = ld, accd
            outs.append(acc * (1.0 / l))
        o_ref[lo:hi, :] = jnp.where(lane < V_HEAD, outs[0], outs[1]).astype(BF16)


def _prompt_attention(q, k, v, *, nb, seq, tq=256):
    head_spec = pl.BlockSpec((None, 2, seq, HEAD_SLAB), lambda b, p: (b, p, 0, 0))
    pair_spec = pl.BlockSpec((None, seq, LANES), lambda b, p: (b, 0, p))
    return pl.pallas_call(
        functools.partial(_prompt_attn_kernel, seq=seq, tq=tq),
        grid=(nb, N_HEADS // 2),
        in_specs=[head_spec, head_spec, pair_spec],
        out_specs=pair_spec,
        out_shape=jax.ShapeDtypeStruct((nb, seq, N_HEADS * V_HEAD), BF16),
        compiler_params=_params(2),
        name="prompt_attention",
    )(q, k, v)


def _decode_attn_kernel(pt_ref, qabs_ref, qr_ref, cnew_ref, krnew_ref, lat_hbm, rope_hbm, olat_ref,
                        latbuf, ropebuf, sems, *, n_pages, group, page):
    b = pl.program_id(0)
    nb = pl.num_programs(0)
    n_groups = n_pages // group
    rows = N_HEADS * SUBLANES

    def page_copies(seq_idx, g, slot):
        out = []
        for p in range(group):
            pid = pt_ref[seq_idx, g * group + p]
            out.append(pltpu.make_async_copy(lat_hbm.at[pid], latbuf.at[slot, pl.ds(p * page, page)], sems.at[0, slot]))
            out.append(pltpu.make_async_copy(rope_hbm.at[pid], ropebuf.at[slot, pl.ds(p * page, page)], sems.at[1, slot]))
        return out

    @pl.when(b == 0)
    def _():
        for cp in page_copies(0, 0, 0):
            cp.start()

    qa = qabs_ref[0].astype(BF16)
    qr = qr_ref[0].astype(BF16)

    zpad = jnp.zeros((SUBLANES, KV_LORA), F32)
    cn = jnp.concatenate([cnew_ref[0], zpad], axis=0).astype(BF16)
    krn = jnp.concatenate([krnew_ref[0], zpad[:, :QK_ROPE]], axis=0).astype(BF16)
    s = (lax.dot_general(qa, cn, NT_DIMS, preferred_element_type=F32)
         + lax.dot_general(qr, krn, NT_DIMS, preferred_element_type=F32))
    t_row = lax.broadcasted_iota(jnp.int32, (rows, 2 * SUBLANES), 0) % SUBLANES
    k_col = lax.broadcasted_iota(jnp.int32, (rows, 2 * SUBLANES), 1)
    s = jnp.where(k_col <= t_row, s, -jnp.inf)
    m = jnp.max(s, axis=-1, keepdims=True)
    p = jnp.exp(s - m)
    l = jnp.sum(p, axis=-1, keepdims=True)
    acc = _mm(p.astype(BF16), cn)

    for g in range(n_groups):
        slot = g % 2
        if g + 1 < n_groups:
            for cp in page_copies(b, g + 1, 1 - slot):
                cp.start()
        else:
            @pl.when(b + 1 < nb)
            def _():
                for cp in page_copies(b + 1, 0, 1 - slot):
                    cp.start()
        for cp in page_copies(b, g, slot):
            cp.wait()
        c = latbuf[slot].astype(BF16)
        kr = ropebuf[slot].astype(BF16)
        s = (lax.dot_general(qa, c, NT_DIMS, preferred_element_type=F32)
             + lax.dot_general(qr, kr, NT_DIMS, preferred_element_type=F32))
        m_new = jnp.maximum(m, jnp.max(s, axis=-1, keepdims=True))
        corr = jnp.exp(m - m_new)
        p = jnp.exp(s - m_new)
        l = l * corr + jnp.sum(p, axis=-1, keepdims=True)
        acc = acc * corr + _mm(p.astype(BF16), c)
        m = m_new

    olat_ref[0] = acc * (1.0 / l)


def _decode_attention(page_table, qabs, qr, c_new, kr_new, cache_lat, cache_rope, *, group=16):
    nb, n_pages = page_table.shape
    page = cache_lat.shape[1]
    rows = N_HEADS * SUBLANES
    assert n_pages % (2 * group) == 0
    per_seq = lambda w, r: pl.BlockSpec((1, r, w), lambda b, pt: (b, 0, 0))
    grid_spec = pltpu.PrefetchScalarGridSpec(
        num_scalar_prefetch=1,
        grid=(nb,),
        in_specs=[per_seq(KV_LORA, rows), per_seq(QK_ROPE, rows), per_seq(KV_LORA, SUBLANES), per_seq(QK_ROPE, SUBLANES),
                  pl.BlockSpec(memory_space=pl.ANY), pl.BlockSpec(memory_space=pl.ANY)],
        out_specs=per_seq(KV_LORA, rows),
        scratch_shapes=[pltpu.VMEM((2, group * page, KV_LORA), F32),
                        pltpu.VMEM((2, group * page, QK_ROPE), F32),
                        pltpu.SemaphoreType.DMA((2, 2))],
    )
    return pl.pallas_call(
        functools.partial(_decode_attn_kernel, n_pages=n_pages, group=group, page=page),
        grid_spec=grid_spec,
        out_shape=jax.ShapeDtypeStruct((nb, rows, KV_LORA), F32),
        compiler_params=_params(),
        name="decode_attention",
    )(page_table, qabs, qr, c_new, kr_new, cache_lat, cache_rope)


def _uv_kernel(olat_ref, wuv_ref, o_ref):
    n = olat_ref.shape[0] * SUBLANES
    for hd in range(N_HEADS):
        x = olat_ref[:, hd].reshape(n, KV_LORA).astype(BF16)
        o_ref[:, hd * V_HEAD:(hd + 1) * V_HEAD] = _mm(x, wuv_ref[hd]).astype(BF16)


def _uv_project(olat, wuv_heads):
    nb = olat.shape[0]
    return pl.pallas_call(
        _uv_kernel,
        grid=(1,),
        in_specs=[pl.BlockSpec(olat.shape, lambda i: (0, 0, 0, 0)),
                  pl.BlockSpec(wuv_heads.shape, lambda i: (0, 0, 0))],
        out_specs=pl.BlockSpec((nb * SUBLANES, N_HEADS * V_HEAD), lambda i: (0, 0)),
        out_shape=jax.ShapeDtypeStruct((nb * SUBLANES, N_HEADS * V_HEAD), BF16),
        compiler_params=_params(),
        name="uv_project",
    )(olat, wuv_heads)


def _mixer_out_kernel(x_ref, sh_ref, sc_ref, g1_ref, o_ref, cb_ref, wg_ref, woa_ref, woc_ref, wo_ref,
                      lng_ref, lnb_ref, x1_ref):
    x = x_ref[...]
    h = (x * (1.0 + sc_ref[...]) + sh_ref[...]).astype(BF16)
    gates = _mm(h, wg_ref[...])
    y_a = _mm(o_ref[...], woa_ref[...])
    y_c = _mm(cb_ref[...], woc_ref[...])
    mixed = _sigmoid(gates[:, :D_MODEL]) * y_a + _sigmoid(gates[:, D_MODEL:]) * y_c
    mix = _mm(mixed.astype(BF16), wo_ref[...])
    x1_ref[...] = _layer_norm(ALPHA * x + g1_ref[...] * mix, lng_ref[...], lnb_ref[...])


def _mixer_out(x2d, sh, sc, g1, o, cb, wts, *, prompt, tm, tiles_per_seq):
    ntok = x2d.shape[0]
    row = lambda w: pl.BlockSpec((tm, w), lambda i: (i, 0))
    ada_spec = (pl.BlockSpec((None, 1, D_MODEL), lambda i: (i // tiles_per_seq, 0, 0)) if prompt else row(D_MODEL))
    return pl.pallas_call(
        _mixer_out_kernel,
        grid=(ntok // tm,),
        in_specs=[row(D_MODEL), ada_spec, ada_spec, ada_spec, row(N_HEADS * V_HEAD), row(CONV_DIM)]
                 + [_const_spec(w.shape) for w in wts],
        out_specs=row(D_MODEL),
        out_shape=jax.ShapeDtypeStruct((ntok, D_MODEL), F32),
        compiler_params=_params(),
        name="mixer_out_prompt" if prompt else "mixer_out_sample",
    )(x2d, sh, sc, g1, o, cb, *wts)


def _ffn_kernel(x1_ref, sh_ref, sc_ref, g2_ref, w1_ref, w3_ref, w2_ref, lng_ref, lnb_ref, y_ref):
    x1 = x1_ref[...]
    h2 = (x1 * (1.0 + sc_ref[...]) + sh_ref[...]).astype(BF16)
    a = _mm(h2, w1_ref[...])
    gated = (a * _sigmoid(a)) * _mm(h2, w3_ref[...])
    f = _mm(gated.astype(BF16), w2_ref[...])
    y_ref[...] = _layer_norm(ALPHA * x1 + g2_ref[...] * f, lng_ref[...], lnb_ref[...])


def _ffn(x1, sh, sc, g2, wts, *, prompt, tm, tiles_per_seq):
    ntok = x1.shape[0]
    row = lambda w: pl.BlockSpec((tm, w), lambda i: (i, 0))
    ada_spec = (pl.BlockSpec((None, 1, D_MODEL), lambda i: (i // tiles_per_seq, 0, 0)) if prompt else row(D_MODEL))
    return pl.pallas_call(
        _ffn_kernel,
        grid=(ntok // tm,),
        in_specs=[row(D_MODEL), ada_spec, ada_spec, ada_spec] + [_const_spec(w.shape) for w in wts],
        out_specs=row(D_MODEL),
        out_shape=jax.ShapeDtypeStruct((ntok, D_MODEL), F32),
        compiler_params=_params(),
        name="ffn_prompt" if prompt else "ffn_sample",
    )(x1, sh, sc, g2, *wts)


def _rope_tables(pos, scale_q):
    inv = ROPE_THETA ** (-jnp.arange(0, QK_ROPE, 2, dtype=F32) / QK_ROPE)
    ang = pos.astype(F32)[:, None] * inv[None, :]
    cos, sin = jnp.cos(ang), jnp.sin(ang)
    n = pos.shape[0]
    z_lo = jnp.zeros((n, ROPE_LO), F32)
    z_hi = jnp.zeros((n, LANES - ROPE_LO - QK_ROPE), F32)
    cos_k = jnp.concatenate([z_lo, cos, cos, z_hi], axis=1)
    sin_k = jnp.concatenate([z_lo, -sin, sin, z_hi], axis=1)
    cos_q = jnp.concatenate([jnp.ones((n, ROPE_LO), F32), cos, cos, z_hi], axis=1) * scale_q
    sin_q = sin_k * scale_q
    return cos_q, sin_q, cos_k, sin_k


def _head_groups(w, width):
    k = w.shape[0]
    return jnp.pad(w, ((0, 0), (0, 0), (0, HEAD_SLAB - width))).reshape(k, N_HEADS * HEAD_SLAB)


def kernel(x_prompt, x_sample, cache_kv_latent, cache_k_rope, state_conv, page_table, c_prompt, c_sample,
           w_ada, b_ada, w_in, q_norm_g, kv_norm_g, w_uq, w_ukv, w_oa, conv_w, w_oc, w_o,
           ln1_g, ln1_b, w_ff1, w_ff3, w_ff2, ln2_g, ln2_b):
    assert w_ada.shape[0] == DEPTH == 1
    nb_p, seq_p, _ = x_prompt.shape
    nb_s, seq_s, _ = x_sample.shape
    assert seq_s == SUBLANES
    n_pool, page = cache_kv_latent.shape[1], cache_kv_latent.shape[2]
    past_len = page_table.shape[1] * page
    layer = 0

    s0, s1, s2, s3 = Q_LORA, Q_LORA + KV_LORA, Q_LORA + KV_LORA + QK_ROPE, Q_LORA + KV_LORA + QK_ROPE + 3 * CONV_DIM
    w_in_l = w_in[layer]
    w_kr_group = jnp.pad(w_in_l[:, s1:s2], ((0, 0), (ROPE_LO, LANES - ROPE_LO - QK_ROPE)))
    wa = jnp.concatenate([w_in_l[:, :s1], w_kr_group], axis=1).astype(BF16)
    wb = w_in_l[:, s2:s3].astype(BF16)
    wg = w_in_l[:, s3:].astype(BF16)
    wuq = _head_groups(w_uq[layer].reshape(Q_LORA, N_HEADS, QK_HEAD), QK_HEAD).astype(BF16)
    w_uk = w_ukv[layer][:, :, :QK_NOPE]
    w_uv = w_ukv[layer][:, :, QK_NOPE:]
    wuk_groups = _head_groups(w_uk, QK_NOPE).astype(BF16)
    wuv_flat = w_uv.reshape(KV_LORA, N_HEADS * V_HEAD).astype(BF16)
    wuv_heads = w_uv.transpose(1, 0, 2).astype(BF16)
    wukt = jnp.pad(w_uk.transpose(1, 2, 0), ((0, 0), (0, HEAD_SLAB - QK_NOPE), (0, 0))).astype(BF16)
    qg = q_norm_g[layer][None, :]
    kvg = kv_norm_g[layer][None, :]
    convw = conv_w[layer]
    in_wts = (wa, wb, qg, kvg, wuq, convw)
    out_wts = (wg, w_oa[layer].astype(BF16), w_oc[layer].astype(BF16), w_o[layer].astype(BF16),
               ln1_g[layer][None, :], ln1_b[layer][None, :])
    ffn_wts = (w_ff1[layer].astype(BF16), w_ff3[layer].astype(BF16), w_ff2[layer].astype(BF16),
               ln2_g[layer][None, :], ln2_b[layer][None, :])

    ada = _ada_terms(jnp.concatenate([c_prompt, c_sample], axis=0), w_ada[layer].astype(BF16), b_ada[layer][None, :])
    ada_p = [a[:, None, :] for a in jnp.split(ada[:nb_p], 6, axis=-1)]
    ada_s = [jnp.repeat(a, seq_s, axis=0) for a in jnp.split(ada[nb_p:], 6, axis=-1)]

    tm_p = 256
    tps = seq_p // tm_p
    xp = x_prompt.reshape(nb_p * seq_p, D_MODEL)
    tabs_p = _rope_tables(jnp.arange(seq_p), ATTN_SCALE)
    ckv_p, kr_p, cb_p, u_p, q_p, k_p, v_p = _mixer_in(
        xp, ada_p[0], ada_p[1], in_wts, tabs_p, prompt=True, nb=nb_p, seq=seq_p, tm=tm_p,
        extra=(wuk_groups, wuv_flat))
    o_p = _prompt_attention(q_p, k_p, v_p.reshape(nb_p, seq_p, N_HEADS * V_HEAD), nb=nb_p, seq=seq_p)
    x1_p = _mixer_out(xp, ada_p[0], ada_p[1], ada_p[2], o_p.reshape(nb_p * seq_p, N_HEADS * V_HEAD), cb_p, out_wts,
                      prompt=True, tm=tm_p, tiles_per_seq=tps)
    y_p = _ffn(x1_p, ada_p[3], ada_p[4], ada_p[5], ffn_wts, prompt=True, tm=tm_p, tiles_per_seq=tps)

    tm_s = 256
    ntok_s = nb_s * seq_s
    xs = x_sample.reshape(ntok_s, D_MODEL)
    tabs_s = tuple(jnp.tile(t, (tm_s // seq_s, 1)) for t in _rope_tables(past_len + jnp.arange(seq_s), ATTN_SCALE))
    st = state_conv[layer]
    zrow = jnp.zeros((nb_s, seq_s - 1, CONV_DIM), F32)
    fix1 = jnp.concatenate([st[:, 1:2], zrow], axis=1).reshape(ntok_s, CONV_DIM)
    fix2 = jnp.concatenate([st, zrow[:, 1:]], axis=1).reshape(ntok_s, CONV_DIM)
    ckv_s, kr_s, cb_s, u_s, qabs_s, qr_s = _mixer_in(
        xs, ada_s[0], ada_s[1], in_wts, tabs_s, prompt=False, nb=nb_s, seq=seq_s, tm=tm_s,
        extra=(wukt, fix1, fix2))
    rows = N_HEADS * seq_s
    olat = _decode_attention(
        page_table, qabs_s.reshape(nb_s, rows, KV_LORA), qr_s.reshape(nb_s, rows, QK_ROPE),
        ckv_s.reshape(nb_s, seq_s, KV_LORA), kr_s.reshape(nb_s, seq_s, QK_ROPE),
        cache_kv_latent.reshape(n_pool, page, KV_LORA), cache_k_rope.reshape(n_pool, page, QK_ROPE))
    o_s = _uv_project(olat.reshape(nb_s, N_HEADS, seq_s, KV_LORA), wuv_heads)
    x1_s = _mixer_out(xs, ada_s[0], ada_s[1], ada_s[2], o_s, cb_s, out_wts, prompt=False, tm=tm_s, tiles_per_seq=1)
    y_s = _ffn(x1_s, ada_s[3], ada_s[4], ada_s[5], ffn_wts, prompt=False, tm=tm_s, tiles_per_seq=1)

    hist = CONV_WIDTH - 1
    return (y_p.reshape(nb_p, seq_p, D_MODEL),
            y_s.reshape(nb_s, seq_s, D_MODEL),
            ckv_p.reshape(1, nb_p, seq_p, KV_LORA),
            kr_p.reshape(1, nb_p, seq_p, QK_ROPE),
            u_p.reshape(1, nb_p, SUBLANES, CONV_DIM)[:, :, SUBLANES - hist:],
            ckv_s.reshape(1, nb_s, seq_s, KV_LORA),
            kr_s.reshape(1, nb_s, seq_s, QK_ROPE),
            u_s.reshape(1, nb_s, seq_s, CONV_DIM)[:, :, seq_s - hist:])
```

```python
import functools
import math

import jax
import jax.numpy as jnp
from jax import lax
from jax.experimental import pallas as pl
from jax.experimental.pallas import tpu as pltpu

D_MODEL = 1024
N_HEADS = 8
QK_NOPE = 64
QK_ROPE = 32
QK_HEAD = QK_NOPE + QK_ROPE
V_HEAD = 64
Q_LORA = 384
KV_LORA = 256
ROPE_THETA = 10000.0
ATTN_SCALE = QK_HEAD ** -0.5
CONV_DIM = 512
CONV_WIDTH = 3
D_FF = int(math.ceil(8 * D_MODEL / 3 / 256)) * 256
DEPTH = 1
ALPHA = (2.0 * DEPTH) ** 0.25
LN_EPS = 1e-5
RMS_EPS = 1e-6

LANES = 128
SUBLANES = 8
HEAD_SLAB = LANES
ROPE_LO = QK_NOPE
ROPE_HALF = QK_ROPE // 2
VMEM_LIMIT = 52 * 1024 * 1024

BF16 = jnp.bfloat16
F32 = jnp.float32
NT_DIMS = (((1,), (1,)), ((), ()))


def _params(n_axes=1, vmem=VMEM_LIMIT):
    return pltpu.CompilerParams(dimension_semantics=("arbitrary",) * n_axes, vmem_limit_bytes=vmem)


def _const_spec(shape):
    nd = len(shape)
    return pl.BlockSpec(shape, lambda *_: (0,) * nd, pipeline_mode=pl.Buffered(1))


def _mm(a, b):
    return jnp.dot(a, b, preferred_element_type=F32)


def _sigmoid(x):
    return 1.0 / (1.0 + jnp.exp(-x))


def _layer_norm(x, g, b):
    mu = jnp.mean(x, axis=-1, keepdims=True)
    xc = x - mu
    var = jnp.mean(xc * xc, axis=-1, keepdims=True)
    return xc * lax.rsqrt(var + LN_EPS) * g + b


def _rms_norm(x, g):
    return x * lax.rsqrt(jnp.mean(x * x, axis=-1, keepdims=True) + RMS_EPS) * g


def _rope_slab(x, cos_t, sin_t):
    lane = lax.broadcasted_iota(jnp.int32, x.shape, 1)
    swapped = jnp.where(lane < ROPE_LO + ROPE_HALF,
                        pltpu.roll(x, LANES - ROPE_HALF, 1), pltpu.roll(x, ROPE_HALF, 1))
    return x * cos_t + swapped * sin_t


def _ada_kernel(c_ref, w_ref, b_ref, o_ref):
    c = c_ref[...]
    s = c * _sigmoid(c)
    o_ref[...] = _mm(s.astype(BF16), w_ref[...]) + b_ref[...]


def _ada_terms(c, w_ada, b_ada):
    n, tn = c.shape[0], 1536
    return pl.pallas_call(
        _ada_kernel,
        grid=(6 * D_MODEL // tn,),
        in_specs=[pl.BlockSpec((n, D_MODEL), lambda j: (0, 0)),
                  pl.BlockSpec((D_MODEL, tn), lambda j: (0, j)),
                  pl.BlockSpec((1, tn), lambda j: (0, j))],
        out_specs=pl.BlockSpec((n, tn), lambda j: (0, j)),
        out_shape=jax.ShapeDtypeStruct((n, 6 * D_MODEL), F32),
        compiler_params=_params(),
        name="ada_terms",
    )(c, w_ada, b_ada)


def _mixer_in_kernel(*refs, prompt, tm, tiles_per_seq):
    if prompt:
        (x_ref, sh_ref, sc_ref, wa_ref, wb_ref, qg_ref, kvg_ref, wuq_ref, convw_ref,
         cq_ref, sq_ref, ck_ref, sk_ref, wuk_ref, wuv_ref,
         ckv_ref, kr_ref, cb_ref, u_ref, q_ref, k_ref, v_ref, ubuf) = refs
    else:
        (x_ref, sh_ref, sc_ref, wa_ref, wb_ref, qg_ref, kvg_ref, wuq_ref, convw_ref,
         cq_ref, sq_ref, ck_ref, sk_ref, wukt_ref, fix1_ref, fix2_ref,
         ckv_ref, kr_ref, cb_ref, u_ref, qabs_ref, qr_ref, ubuf) = refs
    i = pl.program_id(0)

    h = (x_ref[...] * (1.0 + sc_ref[...]) + sh_ref[...]).astype(BF16)

    za = _mm(h, wa_ref[...])
    q_lat = za[:, :Q_LORA]
    kv_lat = za[:, Q_LORA:Q_LORA + KV_LORA]
    kr_slab = za[:, Q_LORA + KV_LORA:]
    c_kv = _rms_norm(kv_lat, kvg_ref[...])
    ckv_ref[...] = c_kv
    kr_rot = _rope_slab(kr_slab, ck_ref[...], sk_ref[...])
    kr_ref[...] = kr_rot[:, ROPE_LO:ROPE_LO + QK_ROPE]

    q_all = _mm(_rms_norm(q_lat, qg_ref[...]).astype(BF16), wuq_ref[...])
    cq, sq = cq_ref[...], sq_ref[...]
    q_heads = [_rope_slab(q_all[:, hd * HEAD_SLAB:(hd + 1) * HEAD_SLAB], cq, sq) for hd in range(N_HEADS)]

    zb = _mm(h, wb_ref[...])
    b_g = zb[:, :CONV_DIM]
    u = zb[:, CONV_DIM:2 * CONV_DIM] * zb[:, 2 * CONV_DIM:]

    @pl.when(i % tiles_per_seq == 0)
    def _():
        ubuf[0:SUBLANES, :] = jnp.zeros((SUBLANES, CONV_DIM), F32)

    ubuf[SUBLANES:SUBLANES + tm, :] = u
    u1 = ubuf[pl.ds(SUBLANES - 1, tm), :]
    u2 = ubuf[pl.ds(SUBLANES - 2, tm), :]
    if prompt:
        ubuf[0:SUBLANES, :] = u[tm - SUBLANES:, :]
        u_ref[...] = u[tm - SUBLANES:, :]
    else:
        t = lax.broadcasted_iota(jnp.int32, (tm, CONV_DIM), 0) % SUBLANES
        u1 = jnp.where(t == 0, fix1_ref[...], u1)
        u2 = jnp.where(t < 2, fix2_ref[...], u2)
        u_ref[...] = u
    cw = convw_ref[...]
    conv_y = u2 * cw[0:1, :] + u1 * cw[1:2, :] + u * cw[2:3, :]
    cb_ref[...] = (b_g * conv_y).astype(BF16)

    if prompt:
        c_bf = c_kv.astype(BF16)
        k_nope = _mm(c_bf, wuk_ref[...])
        v_ref[...] = _mm(c_bf, wuv_ref[...]).astype(BF16)
        for hd in range(N_HEADS):
            q_ref[hd] = q_heads[hd].astype(BF16)
            k_ref[hd] = (k_nope[:, hd * HEAD_SLAB:(hd + 1) * HEAD_SLAB] + kr_rot).astype(BF16)
    else:
        nseq = tm // SUBLANES
        for hd in range(N_HEADS):
            qh = q_heads[hd]
            qabs = _mm(qh.astype(BF16), wukt_ref[hd])
            qabs_ref[:, hd] = qabs.reshape(nseq, SUBLANES, KV_LORA)
            qr_ref[:, hd] = qh[:, ROPE_LO:ROPE_LO + QK_ROPE].reshape(nseq, SUBLANES, QK_ROPE)


def _mixer_in(x2d, sh, sc, wts, tabs, *, prompt, nb, seq, tm, extra):
    ntok = nb * seq
    tiles_per_seq = seq // tm if prompt else 1
    grid = (ntok // tm,)
    row = lambda w: pl.BlockSpec((tm, w), lambda i: (i, 0))
    if prompt:
        ada_spec = pl.BlockSpec((None, 1, D_MODEL), lambda i: (i // tiles_per_seq, 0, 0))
        tab_spec = pl.BlockSpec((tm, LANES), lambda i: (i % tiles_per_seq, 0))
    else:
        ada_spec = row(D_MODEL)
        tab_spec = pl.BlockSpec((tm, LANES), lambda i: (0, 0))
    wa, wb, qg, kvg, wuq, convw = wts
    in_specs = [row(D_MODEL), ada_spec, ada_spec,
                _const_spec(wa.shape), _const_spec(wb.shape), _const_spec(qg.shape), _const_spec(kvg.shape),
                _const_spec(wuq.shape), _const_spec(convw.shape),
                tab_spec, tab_spec, tab_spec, tab_spec]
    out_shape = [jax.ShapeDtypeStruct((ntok, KV_LORA), F32), jax.ShapeDtypeStruct((ntok, QK_ROPE), F32),
                 jax.ShapeDtypeStruct((ntok, CONV_DIM), BF16)]
    out_specs = [row(KV_LORA), row(QK_ROPE), row(CONV_DIM)]
    if prompt:
        wuk, wuv = extra
        in_specs += [_const_spec(wuk.shape), _const_spec(wuv.shape)]
        head_spec = pl.BlockSpec((None, N_HEADS, tm, HEAD_SLAB),
                                 lambda i: (i // tiles_per_seq, 0, i % tiles_per_seq, 0))
        out_shape += [jax.ShapeDtypeStruct((nb * SUBLANES, CONV_DIM), F32),
                      jax.ShapeDtypeStruct((nb, N_HEADS, seq, HEAD_SLAB), BF16),
                      jax.ShapeDtypeStruct((nb, N_HEADS, seq, HEAD_SLAB), BF16),
                      jax.ShapeDtypeStruct((ntok, N_HEADS * V_HEAD), BF16)]
        out_specs += [pl.BlockSpec((SUBLANES, CONV_DIM), lambda i: (i // tiles_per_seq, 0)),
                      head_spec, head_spec, row(N_HEADS * V_HEAD)]
    else:
        wukt, fix1, fix2 = extra
        in_specs += [_const_spec(wukt.shape), row(CONV_DIM), row(CONV_DIM)]
        nseq = tm // SUBLANES
        out_shape += [jax.ShapeDtypeStruct((ntok, CONV_DIM), F32),
                      jax.ShapeDtypeStruct((nb, N_HEADS, SUBLANES, KV_LORA), F32),
                      jax.ShapeDtypeStruct((nb, N_HEADS, SUBLANES, QK_ROPE), F32)]
        out_specs += [row(CONV_DIM),
                      pl.BlockSpec((nseq, N_HEADS, SUBLANES, KV_LORA), lambda i: (i, 0, 0, 0)),
                      pl.BlockSpec((nseq, N_HEADS, SUBLANES, QK_ROPE), lambda i: (i, 0, 0, 0))]
    return pl.pallas_call(
        functools.partial(_mixer_in_kernel, prompt=prompt, tm=tm, tiles_per_seq=tiles_per_seq),
        grid=grid, in_specs=in_specs, out_specs=out_specs, out_shape=out_shape,
        scratch_shapes=[pltpu.VMEM((SUBLANES + tm, CONV_DIM), F32)],
        compiler_params=_params(),
        name="mixer_in_prompt" if prompt else "mixer_in_sample",
    )(x2d, sh, sc, wa, wb, qg, kvg, wuq, convw, *tabs, *extra)


def _prompt_attn_kernel(q_ref, k_ref, v_ref, o_ref, *, seq, tq):
    lane = lax.broadcasted_iota(jnp.int32, (tq, LANES), 1)
    tri = (lax.broadcasted_iota(jnp.int32, (tq, tq), 1) <= lax.broadcasted_iota(jnp.int32, (tq, tq), 0))
    for i in range(seq // tq):
        lo, hi = i * tq, (i + 1) * tq
        outs = []
        for hh in range(2):
            q = q_ref[hh, lo:hi, :]
            sd = lax.dot_general(q, k_ref[hh, lo:hi, :], NT_DIMS, preferred_element_type=F32)
            sd = jnp.where(tri, sd, -jnp.inf)
            m = jnp.max(sd, axis=-1, keepdims=True)
            if i > 0:
                so = lax.dot_general(q, k_ref[hh, 0:lo, :], NT_DIMS, preferred_element_type=F32)
                m = jnp.maximum(m, jnp.max(so, axis=-1, keepdims=True))
                po = jnp.exp(so - m)
                l = jnp.sum(po, axis=-1, keepdims=True)
                acc = _mm(po.astype(BF16), v_ref[0:lo, :])
            pd = jnp.exp(sd - m)
            ld = jnp.sum(pd, axis=-1, keepdims=True)
            accd = _mm(pd.astype(BF16), v_ref[lo:hi, :])
            if i > 0:
                l, acc = l + ld, acc + accd
            else:
                l, acc = ld, accd
            outs.append(acc * (1.0 / l))
        o_ref[lo:hi, :] = jnp.where(lane < V_HEAD, outs[0], outs[1]).astype(BF16)


def _prompt_attention(q, k, v, *, nb, seq, tq=512):
    head_spec = pl.BlockSpec((None, 2, seq, HEAD_SLAB), lambda b, p: (b, p, 0, 0))
    pair_spec = pl.BlockSpec((None, seq, LANES), lambda b, p: (b, 0, p))
    return pl.pallas_call(
        functools.partial(_prompt_attn_kernel, seq=seq, tq=tq),
        grid=(nb, N_HEADS // 2),
        in_specs=[head_spec, head_spec, pair_spec],
        out_specs=pair_spec,
        out_shape=jax.ShapeDtypeStruct((nb, seq, N_HEADS * V_HEAD), BF16),
        compiler_params=_params(2),
        name="prompt_attention",
    )(q, k, v)


def _decode_attn_kernel(pt_ref, qabs_ref, qr_ref, cnew_ref, krnew_ref, lat_hbm, ropet_hbm, olat_ref,
                        latbuf, ropebuf, s_all, sems, *group_bufs, n_pages, group, page, n_slots):
    b = pl.program_id(0)
    nb = pl.num_programs(0)
    n_groups = n_pages // group
    gk = group * page
    rows = N_HEADS * SUBLANES

    def page_copies(seq_idx, g, slot):
        out = []
        for p in range(group):
            pid = pt_ref[seq_idx, g * group + p]
            out.append(pltpu.make_async_copy(lat_hbm.at[pid], latbuf.at[slot, pl.ds(p * page, page)], sems.at[0, slot]))
            out.append(pltpu.make_async_copy(ropet_hbm.at[pid], ropebuf.at[slot, :, pl.ds(p * page, page)], sems.at[1, slot]))
        return out

    @pl.when(b == 0)
    def _():
        for g in range(n_slots):
            for cp in page_copies(0, g, g):
                cp.start()

    qa = qabs_ref[0].astype(BF16)
    qr = qr_ref[0].astype(BF16)

    cbf, krbf = group_bufs[:n_groups], group_bufs[n_groups:]

    def stage(g):
        slot = g % n_slots
        for cp in page_copies(b, g, slot):
            cp.wait()
        cbf[g][...] = latbuf[slot].astype(BF16)
        krbf[g][...] = ropebuf[slot].astype(BF16)

    def refill(g):
        nxt = g + n_slots
        if nxt < n_groups:
            for cp in page_copies(b, nxt, g % n_slots):
                cp.start()
        else:
            @pl.when(b + 1 < nb)
            def _():
                for cp in page_copies(b + 1, nxt - n_groups, g % n_slots):
                    cp.start()

    stage(0)
    for g in range(n_groups):
        if g + 1 < n_groups:
            stage(g + 1)
        s_all[:, g * gk:(g + 1) * gk] = (lax.dot_general(qa, cbf[g][...], NT_DIMS, preferred_element_type=F32)
                                         + _mm(qr, krbf[g][...]))
        refill(g)

    zpad = jnp.zeros((SUBLANES, KV_LORA), F32)
    cn = jnp.concatenate([cnew_ref[0], zpad], axis=0).astype(BF16)
    krn = jnp.concatenate([krnew_ref[0], zpad[:, :QK_ROPE]], axis=0).astype(BF16)
    s_new = (lax.dot_general(qa, cn, NT_DIMS, preferred_element_type=F32)
             + lax.dot_general(qr, krn, NT_DIMS, preferred_element_type=F32))
    t_row = lax.broadcasted_iota(jnp.int32, (rows, 2 * SUBLANES), 0) % SUBLANES
    k_col = lax.broadcasted_iota(jnp.int32, (rows, 2 * SUBLANES), 1)
    s_new = jnp.where(k_col <= t_row, s_new, -jnp.inf)

    m = jnp.maximum(jnp.max(s_new, axis=-1, keepdims=True), jnp.max(s_all[...], axis=-1, keepdims=True))
    p = jnp.exp(s_new - m)
    l = jnp.sum(p, axis=-1, keepdims=True)
    acc = _mm(p.astype(BF16), cn)
    for g in range(n_groups):
        p = jnp.exp(s_all[:, g * gk:(g + 1) * gk] - m)
        l = l + jnp.sum(p, axis=-1, keepdims=True)
        acc = acc + _mm(p.astype(BF16), cbf[g][...])
    olat_ref[0] = acc * (1.0 / l)


def _decode_attention(page_table, qabs, qr, c_new, kr_new, cache_lat, cache_rope_t, *, group=16, n_slots=4):
    nb, n_pages = page_table.shape
    page = cache_lat.shape[1]
    rows = N_HEADS * SUBLANES
    n_groups = n_pages // group
    assert n_pages % group == 0 and n_groups % n_slots == 0 and n_groups % 2 == 0
    per_seq = lambda w, r: pl.BlockSpec((1, r, w), lambda b, pt: (b, 0, 0))
    grid_spec = pltpu.PrefetchScalarGridSpec(
        num_scalar_prefetch=1,
        grid=(nb,),
        in_specs=[per_seq(KV_LORA, rows), per_seq(QK_ROPE, rows), per_seq(KV_LORA, SUBLANES), per_seq(QK_ROPE, SUBLANES),
                  pl.BlockSpec(memory_space=pl.ANY), pl.BlockSpec(memory_space=pl.ANY)],
        out_specs=per_seq(KV_LORA, rows),
        scratch_shapes=[pltpu.VMEM((n_slots, group * page, KV_LORA), F32),
                        pltpu.VMEM((n_slots, QK_ROPE, group * page), F32),
                        pltpu.VMEM((rows, n_pages * page), F32),
                        pltpu.SemaphoreType.DMA((2, n_slots))]
                       + [pltpu.VMEM((group * page, KV_LORA), BF16)] * n_groups
                       + [pltpu.VMEM((QK_ROPE, group * page), BF16)] * n_groups,
    )
    return pl.pallas_call(
        functools.partial(_decode_attn_kernel, n_pages=n_pages, group=group, page=page, n_slots=n_slots),
        grid_spec=grid_spec,
        out_shape=jax.ShapeDtypeStruct((nb, rows, KV_LORA), F32),
        compiler_params=_params(),
        name="decode_attention",
    )(page_table, qabs, qr, c_new, kr_new, cache_lat, cache_rope_t)


def _uv_kernel(olat_ref, wuv_ref, o_ref):
    n = olat_ref.shape[0] * SUBLANES
    for hd in range(N_HEADS):
        x = olat_ref[:, hd].reshape(n, KV_LORA).astype(BF16)
        o_ref[:, hd * V_HEAD:(hd + 1) * V_HEAD] = _mm(x, wuv_ref[hd]).astype(BF16)


def _uv_project(olat, wuv_heads):
    nb = olat.shape[0]
    return pl.pallas_call(
        _uv_kernel,
        grid=(1,),
        in_specs=[pl.BlockSpec(olat.shape, lambda i: (0, 0, 0, 0)),
                  pl.BlockSpec(wuv_heads.shape, lambda i: (0, 0, 0))],
        out_specs=pl.BlockSpec((nb * SUBLANES, N_HEADS * V_HEAD), lambda i: (0, 0)),
        out_shape=jax.ShapeDtypeStruct((nb * SUBLANES, N_HEADS * V_HEAD), BF16),
        compiler_params=_params(),
        name="uv_project",
    )(olat, wuv_heads)


def _mixer_out_kernel(x_ref, sh_ref, sc_ref, g1_ref, o_ref, cb_ref, wg_ref, woa_ref, woc_ref, wo_ref,
                      lng_ref, lnb_ref, x1_ref):
    x = x_ref[...]
    h = (x * (1.0 + sc_ref[...]) + sh_ref[...]).astype(BF16)
    gates = _mm(h, wg_ref[...])
    y_a = _mm(o_ref[...], woa_ref[...])
    y_c = _mm(cb_ref[...], woc_ref[...])
    mixed = _sigmoid(gates[:, :D_MODEL]) * y_a + _sigmoid(gates[:, D_MODEL:]) * y_c
    mix = _mm(mixed.astype(BF16), wo_ref[...])
    x1_ref[...] = _layer_norm(ALPHA * x + g1_ref[...] * mix, lng_ref[...], lnb_ref[...])


def _mixer_out(x2d, sh, sc, g1, o, cb, wts, *, prompt, tm, tiles_per_seq):
    ntok = x2d.shape[0]
    row = lambda w: pl.BlockSpec((tm, w), lambda i: (i, 0))
    ada_spec = (pl.BlockSpec((None, 1, D_MODEL), lambda i: (i // tiles_per_seq, 0, 0)) if prompt else row(D_MODEL))
    return pl.pallas_call(
        _mixer_out_kernel,
        grid=(ntok // tm,),
        in_specs=[row(D_MODEL), ada_spec, ada_spec, ada_spec, row(N_HEADS * V_HEAD), row(CONV_DIM)]
                 + [_const_spec(w.shape) for w in wts],
        out_specs=row(D_MODEL),
        out_shape=jax.ShapeDtypeStruct((ntok, D_MODEL), F32),
        compiler_params=_params(),
        name="mixer_out_prompt" if prompt else "mixer_out_sample",
    )(x2d, sh, sc, g1, o, cb, *wts)


def _ffn_kernel(x1_ref, sh_ref, sc_ref, g2_ref, w1_ref, w3_ref, w2_ref, lng_ref, lnb_ref, y_ref):
    x1 = x1_ref[...]
    h2 = (x1 * (1.0 + sc_ref[...]) + sh_ref[...]).astype(BF16)
    a = _mm(h2, w1_ref[...])
    gated = (a * _sigmoid(a)) * _mm(h2, w3_ref[...])
    f = _mm(gated.astype(BF16), w2_ref[...])
    y_ref[...] = _layer_norm(ALPHA * x1 + g2_ref[...] * f, lng_ref[...], lnb_ref[...])


def _ffn(x1, sh, sc, g2, wts, *, prompt, tm, tiles_per_seq):
    ntok = x1.shape[0]
    row = lambda w: pl.BlockSpec((tm, w), lambda i: (i, 0))
    ada_spec = (pl.BlockSpec((None, 1, D_MODEL), lambda i: (i // tiles_per_seq, 0, 0)) if prompt else row(D_MODEL))
    return pl.pallas_call(
        _ffn_kernel,
        grid=(ntok // tm,),
        in_specs=[row(D_MODEL), ada_spec, ada_spec, ada_spec] + [_const_spec(w.shape) for w in wts],
        out_specs=row(D_MODEL),
        out_shape=jax.ShapeDtypeStruct((ntok, D_MODEL), F32),
        compiler_params=_params(),
        name="ffn_prompt" if prompt else "ffn_sample",
    )(x1, sh, sc, g2, *wts)


def _rope_tables(pos, scale_q):
    inv = ROPE_THETA ** (-jnp.arange(0, QK_ROPE, 2, dtype=F32) / QK_ROPE)
    ang = pos.astype(F32)[:, None] * inv[None, :]
    cos, sin = jnp.cos(ang), jnp.sin(ang)
    n = pos.shape[0]
    z_lo = jnp.zeros((n, ROPE_LO), F32)
    z_hi = jnp.zeros((n, LANES - ROPE_LO - QK_ROPE), F32)
    cos_k = jnp.concatenate([z_lo, cos, cos, z_hi], axis=1)
    sin_k = jnp.concatenate([z_lo, -sin, sin, z_hi], axis=1)
    cos_q = jnp.concatenate([jnp.ones((n, ROPE_LO), F32), cos, cos, z_hi], axis=1) * scale_q
    sin_q = sin_k * scale_q
    return cos_q, sin_q, cos_k, sin_k


def _head_groups(w, width):
    k = w.shape[0]
    return jnp.pad(w, ((0, 0), (0, 0), (0, HEAD_SLAB - width))).reshape(k, N_HEADS * HEAD_SLAB)


def kernel(x_prompt, x_sample, cache_kv_latent, cache_k_rope, state_conv, page_table, c_prompt, c_sample,
           w_ada, b_ada, w_in, q_norm_g, kv_norm_g, w_uq, w_ukv, w_oa, conv_w, w_oc, w_o,
           ln1_g, ln1_b, w_ff1, w_ff3, w_ff2, ln2_g, ln2_b):
    assert w_ada.shape[0] == DEPTH == 1
    nb_p, seq_p, _ = x_prompt.shape
    nb_s, seq_s, _ = x_sample.shape
    assert seq_s == SUBLANES
    n_pool, page = cache_kv_latent.shape[1], cache_kv_latent.shape[2]
    past_len = page_table.shape[1] * page
    layer = 0

    s0, s1, s2, s3 = Q_LORA, Q_LORA + KV_LORA, Q_LORA + KV_LORA + QK_ROPE, Q_LORA + KV_LORA + QK_ROPE + 3 * CONV_DIM
    w_in_l = w_in[layer]
    w_kr_group = jnp.pad(w_in_l[:, s1:s2], ((0, 0), (ROPE_LO, LANES - ROPE_LO - QK_ROPE)))
    wa = jnp.concatenate([w_in_l[:, :s1], w_kr_group], axis=1).astype(BF16)
    wb = w_in_l[:, s2:s3].astype(BF16)
    wg = w_in_l[:, s3:].astype(BF16)
    wuq = _head_groups(w_uq[layer].reshape(Q_LORA, N_HEADS, QK_HEAD), QK_HEAD).astype(BF16)
    w_uk = w_ukv[layer][:, :, :QK_NOPE]
    w_uv = w_ukv[layer][:, :, QK_NOPE:]
    wuk_groups = _head_groups(w_uk, QK_NOPE).astype(BF16)
    wuv_flat = w_uv.reshape(KV_LORA, N_HEADS * V_HEAD).astype(BF16)
    wuv_heads = w_uv.transpose(1, 0, 2).astype(BF16)
    wukt = jnp.pad(w_uk.transpose(1, 2, 0), ((0, 0), (0, HEAD_SLAB - QK_NOPE), (0, 0))).astype(BF16)
    qg = q_norm_g[layer][None, :]
    kvg = kv_norm_g[layer][None, :]
    convw = conv_w[layer]
    in_wts = (wa, wb, qg, kvg, wuq, convw)
    out_wts = (wg, w_oa[layer].astype(BF16), w_oc[layer].astype(BF16), w_o[layer].astype(BF16),
               ln1_g[layer][None, :], ln1_b[layer][None, :])
    ffn_wts = (w_ff1[layer].astype(BF16), w_ff3[layer].astype(BF16), w_ff2[layer].astype(BF16),
               ln2_g[layer][None, :], ln2_b[layer][None, :])

    ada = _ada_terms(jnp.concatenate([c_prompt, c_sample], axis=0), w_ada[layer].astype(BF16), b_ada[layer][None, :])
    ada_p = [a[:, None, :] for a in jnp.split(ada[:nb_p], 6, axis=-1)]
    ada_s = [jnp.repeat(a, seq_s, axis=0) for a in jnp.split(ada[nb_p:], 6, axis=-1)]

    tm_p = 256
    tps = seq_p // tm_p
    xp = x_prompt.reshape(nb_p * seq_p, D_MODEL)
    tabs_p = _rope_tables(jnp.arange(seq_p), ATTN_SCALE)
    ckv_p, kr_p, cb_p, u_p, q_p, k_p, v_p = _mixer_in(
        xp, ada_p[0], ada_p[1], in_wts, tabs_p, prompt=True, nb=nb_p, seq=seq_p, tm=tm_p,
        extra=(wuk_groups, wuv_flat))
    o_p = _prompt_attention(q_p, k_p, v_p.reshape(nb_p, seq_p, N_HEADS * V_HEAD), nb=nb_p, seq=seq_p)
    x1_p = _mixer_out(xp, ada_p[0], ada_p[1], ada_p[2], o_p.reshape(nb_p * seq_p, N_HEADS * V_HEAD), cb_p, out_wts,
                      prompt=True, tm=tm_p, tiles_per_seq=tps)
    y_p = _ffn(x1_p, ada_p[3], ada_p[4], ada_p[5], ffn_wts, prompt=True, tm=tm_p, tiles_per_seq=tps)

    tm_s = 256
    ntok_s = nb_s * seq_s
    xs = x_sample.reshape(ntok_s, D_MODEL)
    tabs_s = tuple(jnp.tile(t, (tm_s // seq_s, 1)) for t in _rope_tables(past_len + jnp.arange(seq_s), ATTN_SCALE))
    st = state_conv[layer]
    zrow = jnp.zeros((nb_s, seq_s - 1, CONV_DIM), F32)
    fix1 = jnp.concatenate([st[:, 1:2], zrow], axis=1).reshape(ntok_s, CONV_DIM)
    fix2 = jnp.concatenate([st, zrow[:, 1:]], axis=1).reshape(ntok_s, CONV_DIM)
    ckv_s, kr_s, cb_s, u_s, qabs_s, qr_s = _mixer_in(
        xs, ada_s[0], ada_s[1], in_wts, tabs_s, prompt=False, nb=nb_s, seq=seq_s, tm=tm_s,
        extra=(wukt, fix1, fix2))
    rows = N_HEADS * seq_s
    olat = _decode_attention(
        page_table, qabs_s.reshape(nb_s, rows, KV_LORA), qr_s.reshape(nb_s, rows, QK_ROPE),
        ckv_s.reshape(nb_s, seq_s, KV_LORA), kr_s.reshape(nb_s, seq_s, QK_ROPE),
        cache_kv_latent.reshape(n_pool, page, KV_LORA),
        jnp.swapaxes(cache_k_rope.reshape(n_pool, page, QK_ROPE), 1, 2))
    o_s = _uv_project(olat.reshape(nb_s, N_HEADS, seq_s, KV_LORA), wuv_heads)
    x1_s = _mixer_out(xs, ada_s[0], ada_s[1], ada_s[2], o_s, cb_s, out_wts, prompt=False, tm=tm_s, tiles_per_seq=1)
    y_s = _ffn(x1_s, ada_s[3], ada_s[4], ada_s[5], ffn_wts, prompt=False, tm=tm_s, tiles_per_seq=1)

    hist = CONV_WIDTH - 1
    return (y_p.reshape(nb_p, seq_p, D_MODEL),
            y_s.reshape(nb_s, seq_s, D_MODEL),
            ckv_p.reshape(1, nb_p, seq_p, KV_LORA),
            kr_p.reshape(1, nb_p, seq_p, QK_ROPE),
            u_p.reshape(1, nb_p, SUBLANES, CONV_DIM)[:, :, SUBLANES - hist:],
            ckv_s.reshape(1, nb_s, seq_s, KV_LORA),
            kr_s.reshape(1, nb_s, seq_s, QK_ROPE),
            u_s.reshape(1, nb_s, seq_s, CONV_DIM)[:, :, seq_s - hist:])
```

```python
import functools
import math

import jax
import jax.numpy as jnp
from jax import lax
from jax.experimental import pallas as pl
from jax.experimental.pallas import tpu as pltpu

D_MODEL = 1024
N_HEADS = 8
QK_NOPE = 64
QK_ROPE = 32
QK_HEAD = QK_NOPE + QK_ROPE
V_HEAD = 64
Q_LORA = 384
KV_LORA = 256
ROPE_THETA = 10000.0
ATTN_SCALE = QK_HEAD ** -0.5
CONV_DIM = 512
CONV_WIDTH = 3
D_FF = int(math.ceil(8 * D_MODEL / 3 / 256)) * 256
DEPTH = 1
ALPHA = (2.0 * DEPTH) ** 0.25
LN_EPS = 1e-5
RMS_EPS = 1e-6

LANES = 128
SUBLANES = 8
HEAD_SLAB = LANES
ROPE_LO = QK_NOPE
ROPE_HALF = QK_ROPE // 2
VMEM_LIMIT = 52 * 1024 * 1024

BF16 = jnp.bfloat16
F32 = jnp.float32
NT_DIMS = (((1,), (1,)), ((), ()))


def _params(n_axes=1, vmem=VMEM_LIMIT):
    return pltpu.CompilerParams(dimension_semantics=("arbitrary",) * n_axes, vmem_limit_bytes=vmem)


def _const_spec(shape):
    nd = len(shape)
    return pl.BlockSpec(shape, lambda *_: (0,) * nd, pipeline_mode=pl.Buffered(1))


def _mm(a, b):
    return jnp.dot(a, b, preferred_element_type=F32)


def _sigmoid(x):
    return 1.0 / (1.0 + jnp.exp(-x))


def _layer_norm(x, g, b):
    mu = jnp.mean(x, axis=-1, keepdims=True)
    xc = x - mu
    var = jnp.mean(xc * xc, axis=-1, keepdims=True)
    return xc * lax.rsqrt(var + LN_EPS) * g + b


def _rms_norm(x, g):
    return x * lax.rsqrt(jnp.mean(x * x, axis=-1, keepdims=True) + RMS_EPS) * g


def _rope_slab(x, cos_t, sin_t):
    lane = lax.broadcasted_iota(jnp.int32, x.shape, 1)
    swapped = jnp.where(lane < ROPE_LO + ROPE_HALF,
                        pltpu.roll(x, LANES - ROPE_HALF, 1), pltpu.roll(x, ROPE_HALF, 1))
    return x * cos_t + swapped * sin_t


def _ada_kernel(c_ref, w_ref, b_ref, o_ref):
    c = c_ref[...]
    s = c * _sigmoid(c)
    o_ref[...] = _mm(s.astype(BF16), w_ref[...]) + b_ref[...]


def _ada_terms(c, w_ada, b_ada):
    n, tn = c.shape[0], 1536
    return pl.pallas_call(
        _ada_kernel,
        grid=(6 * D_MODEL // tn,),
        in_specs=[pl.BlockSpec((n, D_MODEL), lambda j: (0, 0)),
                  pl.BlockSpec((D_MODEL, tn), lambda j: (0, j)),
                  pl.BlockSpec((1, tn), lambda j: (0, j))],
        out_specs=pl.BlockSpec((n, tn), lambda j: (0, j)),
        out_shape=jax.ShapeDtypeStruct((n, 6 * D_MODEL), F32),
        compiler_params=_params(),
        name="ada_terms",
    )(c, w_ada, b_ada)


def _mixer_in_kernel(*refs, prompt, tm, tiles_per_seq):
    if prompt:
        (x_ref, sh_ref, sc_ref, wa_ref, wb_ref, qg_ref, kvg_ref, wuq_ref, convw_ref,
         cq_ref, sq_ref, ck_ref, sk_ref, wuk_ref, wuv_ref,
         ckv_ref, kr_ref, cb_ref, u_ref, q_ref, k_ref, v_ref, ubuf) = refs
    else:
        (x_ref, sh_ref, sc_ref, wa_ref, wb_ref, qg_ref, kvg_ref, wuq_ref, convw_ref,
         cq_ref, sq_ref, ck_ref, sk_ref, wukt_ref, fix1_ref, fix2_ref,
         ckv_ref, kr_ref, cb_ref, u_ref, qabs_ref, qr_ref, ubuf) = refs
    i = pl.program_id(0)

    h = (x_ref[...] * (1.0 + sc_ref[...]) + sh_ref[...]).astype(BF16)

    za = _mm(h, wa_ref[...])
    q_lat = za[:, :Q_LORA]
    kv_lat = za[:, Q_LORA:Q_LORA + KV_LORA]
    kr_slab = za[:, Q_LORA + KV_LORA:]
    c_kv = _rms_norm(kv_lat, kvg_ref[...])
    ckv_ref[...] = c_kv
    kr_rot = _rope_slab(kr_slab, ck_ref[...], sk_ref[...])
    kr_ref[...] = kr_rot[:, ROPE_LO:ROPE_LO + QK_ROPE]

    q_all = _mm(_rms_norm(q_lat, qg_ref[...]).astype(BF16), wuq_ref[...])
    cq, sq = cq_ref[...], sq_ref[...]
    q_heads = [_rope_slab(q_all[:, hd * HEAD_SLAB:(hd + 1) * HEAD_SLAB], cq, sq) for hd in range(N_HEADS)]

    zb = _mm(h, wb_ref[...])
    b_g = zb[:, :CONV_DIM]
    u = zb[:, CONV_DIM:2 * CONV_DIM] * zb[:, 2 * CONV_DIM:]

    @pl.when(i % tiles_per_seq == 0)
    def _():
        ubuf[0:SUBLANES, :] = jnp.zeros((SUBLANES, CONV_DIM), F32)

    ubuf[SUBLANES:SUBLANES + tm, :] = u
    u1 = ubuf[pl.ds(SUBLANES - 1, tm), :]
    u2 = ubuf[pl.ds(SUBLANES - 2, tm), :]
    if prompt:
        ubuf[0:SUBLANES, :] = u[tm - SUBLANES:, :]
        u_ref[...] = u[tm - SUBLANES:, :]
    else:
        t = lax.broadcasted_iota(jnp.int32, (tm, CONV_DIM), 0) % SUBLANES
        u1 = jnp.where(t == 0, fix1_ref[...], u1)
        u2 = jnp.where(t < 2, fix2_ref[...], u2)
        u_ref[...] = u
    cw = convw_ref[...]
    conv_y = u2 * cw[0:1, :] + u1 * cw[1:2, :] + u * cw[2:3, :]
    cb_ref[...] = (b_g * conv_y).astype(BF16)

    if prompt:
        c_bf = c_kv.astype(BF16)
        k_nope = _mm(c_bf, wuk_ref[...])
        v_ref[...] = _mm(c_bf, wuv_ref[...]).astype(BF16)
        for hd in range(N_HEADS):
            q_ref[hd] = q_heads[hd].astype(BF16)
            k_ref[hd] = (k_nope[:, hd * HEAD_SLAB:(hd + 1) * HEAD_SLAB] + kr_rot).astype(BF16)
    else:
        nseq = tm // SUBLANES
        for hd in range(N_HEADS):
            qh = q_heads[hd]
            qabs = _mm(qh.astype(BF16), wukt_ref[hd])
            qabs_ref[:, hd] = qabs.reshape(nseq, SUBLANES, KV_LORA)
            qr_ref[:, hd] = qh[:, ROPE_LO:ROPE_LO + QK_ROPE].reshape(nseq, SUBLANES, QK_ROPE)


def _mixer_in(x2d, sh, sc, wts, tabs, *, prompt, nb, seq, tm, extra):
    ntok = nb * seq
    tiles_per_seq = seq // tm if prompt else 1
    grid = (ntok // tm,)
    row = lambda w: pl.BlockSpec((tm, w), lambda i: (i, 0))
    if prompt:
        ada_spec = pl.BlockSpec((None, 1, D_MODEL), lambda i: (i // tiles_per_seq, 0, 0))
        tab_spec = pl.BlockSpec((tm, LANES), lambda i: (i % tiles_per_seq, 0))
    else:
        ada_spec = row(D_MODEL)
        tab_spec = pl.BlockSpec((tm, LANES), lambda i: (0, 0))
    wa, wb, qg, kvg, wuq, convw = wts
    in_specs = [row(D_MODEL), ada_spec, ada_spec,
                _const_spec(wa.shape), _const_spec(wb.shape), _const_spec(qg.shape), _const_spec(kvg.shape),
                _const_spec(wuq.shape), _const_spec(convw.shape),
                tab_spec, tab_spec, tab_spec, tab_spec]
    out_shape = [jax.ShapeDtypeStruct((ntok, KV_LORA), F32), jax.ShapeDtypeStruct((ntok, QK_ROPE), F32),
                 jax.ShapeDtypeStruct((ntok, CONV_DIM), BF16)]
    out_specs = [row(KV_LORA), row(QK_ROPE), row(CONV_DIM)]
    if prompt:
        wuk, wuv = extra
        in_specs += [_const_spec(wuk.shape), _const_spec(wuv.shape)]
        head_spec = pl.BlockSpec((None, N_HEADS, tm, HEAD_SLAB),
                                 lambda i: (i // tiles_per_seq, 0, i % tiles_per_seq, 0))
        out_shape += [jax.ShapeDtypeStruct((nb * SUBLANES, CONV_DIM), F32),
                      jax.ShapeDtypeStruct((nb, N_HEADS, seq, HEAD_SLAB), BF16),
                      jax.ShapeDtypeStruct((nb, N_HEADS, seq, HEAD_SLAB), BF16),
                      jax.ShapeDtypeStruct((ntok, N_HEADS * V_HEAD), BF16)]
        out_specs += [pl.BlockSpec((SUBLANES, CONV_DIM), lambda i: (i // tiles_per_seq, 0)),
                      head_spec, head_spec, row(N_HEADS * V_HEAD)]
    else:
        wukt, fix1, fix2 = extra
        in_specs += [_const_spec(wukt.shape), row(CONV_DIM), row(CONV_DIM)]
        nseq = tm // SUBLANES
        out_shape += [jax.ShapeDtypeStruct((ntok, CONV_DIM), F32),
                      jax.ShapeDtypeStruct((nb, N_HEADS, SUBLANES, KV_LORA), F32),
                      jax.ShapeDtypeStruct((nb, N_HEADS, SUBLANES, QK_ROPE), F32)]
        out_specs += [row(CONV_DIM),
                      pl.BlockSpec((nseq, N_HEADS, SUBLANES, KV_LORA), lambda i: (i, 0, 0, 0)),
                      pl.BlockSpec((nseq, N_HEADS, SUBLANES, QK_ROPE), lambda i: (i, 0, 0, 0))]
    return pl.pallas_call(
        functools.partial(_mixer_in_kernel, prompt=prompt, tm=tm, tiles_per_seq=tiles_per_seq),
        grid=grid, in_specs=in_specs, out_specs=out_specs, out_shape=out_shape,
        scratch_shapes=[pltpu.VMEM((SUBLANES + tm, CONV_DIM), F32)],
        compiler_params=_params(),
        name="mixer_in_prompt" if prompt else "mixer_in_sample",
    )(x2d, sh, sc, wa, wb, qg, kvg, wuq, convw, *tabs, *extra)


def _prompt_attn_kernel(q_ref, k_ref, v_ref, o_ref, *, seq, tq):
    lane = lax.broadcasted_iota(jnp.int32, (tq, LANES), 1)
    tri = (lax.broadcasted_iota(jnp.int32, (tq, tq), 1) <= lax.broadcasted_iota(jnp.int32, (tq, tq), 0))
    for i in range(seq // tq):
        lo, hi = i * tq, (i + 1) * tq
        outs = []
        for hh in range(2):
            q = q_ref[hh, lo:hi, :]
            sd = lax.dot_general(q, k_ref[hh, lo:hi, :], NT_DIMS, preferred_element_type=F32)
            sd = jnp.where(tri, sd, -jnp.inf)
            m = jnp.max(sd, axis=-1, keepdims=True)
            if i > 0:
                so = lax.dot_general(q, k_ref[hh, 0:lo, :], NT_DIMS, preferred_element_type=F32)
                m = jnp.maximum(m, jnp.max(so, axis=-1, keepdims=True))
                po = jnp.exp(so - m)
                l = jnp.sum(po, axis=-1, keepdims=True)
                acc = _mm(po.astype(BF16), v_ref[0:lo, :])
            pd = jnp.exp(sd - m)
            ld = jnp.sum(pd, axis=-1, keepdims=True)
            accd = _mm(pd.astype(BF16), v_ref[lo:hi, :])
            if i > 0:
                l, acc = l + ld, acc + accd
            else:
                l, acc = ld, accd
            outs.append(acc * (1.0 / l))
        o_ref[lo:hi, :] = jnp.where(lane < V_HEAD, outs[0], outs[1]).astype(BF16)


def _prompt_attention(q, k, v, *, nb, seq, tq=512):
    head_spec = pl.BlockSpec((None, 2, seq, HEAD_SLAB), lambda b, p: (b, p, 0, 0))
    pair_spec = pl.BlockSpec((None, seq, LANES), lambda b, p: (b, 0, p))
    return pl.pallas_call(
        functools.partial(_prompt_attn_kernel, seq=seq, tq=tq),
        grid=(nb, N_HEADS // 2),
        in_specs=[head_spec, head_spec, pair_spec],
        out_specs=pair_spec,
        out_shape=jax.ShapeDtypeStruct((nb, seq, N_HEADS * V_HEAD), BF16),
        compiler_params=_params(2),
        name="prompt_attention",
    )(q, k, v)


def _decode_attn_kernel(pt_ref, qabs_ref, qr_ref, cnew_ref, krnew_ref, lat_hbm, ropet_hbm, olat_ref,
                        latbuf, ropebuf, sems, s_all, s_new_buf, *group_bufs,
                        n_pages, group, page, n_slots, n_seq):
    b = pl.program_id(0)
    n_groups = n_pages // group
    gk = group * page
    rows = N_HEADS * SUBLANES
    cbf, krbf = group_bufs[:n_groups], group_bufs[n_groups:]

    def page_copies(seq_idx, g, slot):
        out = []
        for p in range(group):
            pid = pt_ref[seq_idx, g * group + p]
            out.append(pltpu.make_async_copy(lat_hbm.at[pid], latbuf.at[slot, pl.ds(p * page, page)], sems.at[0, slot]))
            out.append(pltpu.make_async_copy(ropet_hbm.at[pid], ropebuf.at[slot, :, pl.ds(p * page, page)], sems.at[1, slot]))
        return out

    @pl.when(b == 0)
    def _():
        for g in range(n_slots):
            for cp in page_copies(0, g, g):
                cp.start()

    def new_token_block(ref, width):
        return jnp.concatenate([ref[0], jnp.zeros((SUBLANES, width), F32)], axis=0).astype(BF16)

    def stage(g):
        slot = g % n_slots
        for cp in page_copies(b, g, slot):
            cp.wait()
        cbf[g][...] = latbuf[slot].astype(BF16)
        krbf[g][...] = ropebuf[slot].astype(BF16)

    def refill(g):
        nxt = g + n_slots
        if nxt < n_groups:
            for cp in page_copies(b, nxt, g % n_slots):
                cp.start()
        else:
            @pl.when(b + 1 < n_seq)
            def _():
                for cp in page_copies(b + 1, nxt - n_groups, g % n_slots):
                    cp.start()

    def phase1():
        qa = qabs_ref[0].astype(BF16)
        qr = qr_ref[0].astype(BF16)
        s_new = (lax.dot_general(qa, new_token_block(cnew_ref, KV_LORA), NT_DIMS, preferred_element_type=F32)
                 + lax.dot_general(qr, new_token_block(krnew_ref, QK_ROPE), NT_DIMS, preferred_element_type=F32))
        t_row = lax.broadcasted_iota(jnp.int32, (rows, 2 * SUBLANES), 0) % SUBLANES
        k_col = lax.broadcasted_iota(jnp.int32, (rows, 2 * SUBLANES), 1)
        s_new_buf[...] = jnp.where(k_col <= t_row, s_new, -jnp.inf)
        stage(0)
        for g in range(n_groups):
            if g + 1 < n_groups:
                stage(g + 1)
            s_all[:, g * gk:(g + 1) * gk] = (lax.dot_general(qa, cbf[g][...], NT_DIMS, preferred_element_type=F32)
                                             + _mm(qr, krbf[g][...]))
            refill(g)

    def phase2():
        cn = new_token_block(cnew_ref, KV_LORA)
        sn = s_new_buf[...]
        m = jnp.maximum(jnp.max(sn, axis=-1, keepdims=True), jnp.max(s_all[...], axis=-1, keepdims=True))
        p = jnp.exp(sn - m)
        l = jnp.sum(p, axis=-1, keepdims=True)
        acc = _mm(p.astype(BF16), cn)
        for g in range(n_groups):
            p = jnp.exp(s_all[:, g * gk:(g + 1) * gk] - m)
            l = l + jnp.sum(p, axis=-1, keepdims=True)
            acc = acc + _mm(p.astype(BF16), cbf[g][...])
        olat_ref[0] = acc * (1.0 / l)

    pl.when(b >= 0)(phase1)
    pl.when(b < n_seq)(phase2)


def _decode_attention(page_table, qabs, qr, c_new, kr_new, cache_lat, cache_rope_t, *, group=16, n_slots=4):
    nb, n_pages = page_table.shape
    page = cache_lat.shape[1]
    rows = N_HEADS * SUBLANES
    n_groups = n_pages // group
    assert n_pages % group == 0 and n_groups % n_slots == 0
    per_seq = lambda w, r: pl.BlockSpec((1, r, w), lambda b, pt: (b, 0, 0))
    grid_spec = pltpu.PrefetchScalarGridSpec(
        num_scalar_prefetch=1,
        grid=(nb,),
        in_specs=[per_seq(KV_LORA, rows), per_seq(QK_ROPE, rows), per_seq(KV_LORA, SUBLANES), per_seq(QK_ROPE, SUBLANES),
                  pl.BlockSpec(memory_space=pl.ANY), pl.BlockSpec(memory_space=pl.ANY)],
        out_specs=per_seq(KV_LORA, rows),
        scratch_shapes=[pltpu.VMEM((n_slots, group * page, KV_LORA), F32),
                        pltpu.VMEM((n_slots, QK_ROPE, group * page), F32),
                        pltpu.SemaphoreType.DMA((2, n_slots)),
                        pltpu.VMEM((rows, n_pages * page), F32),
                        pltpu.VMEM((rows, 2 * SUBLANES), F32)]
                       + [pltpu.VMEM((group * page, KV_LORA), BF16)] * n_groups
                       + [pltpu.VMEM((QK_ROPE, group * page), BF16)] * n_groups,
    )
    return pl.pallas_call(
        functools.partial(_decode_attn_kernel, n_pages=n_pages, group=group, page=page, n_slots=n_slots, n_seq=nb),
        grid_spec=grid_spec,
        out_shape=jax.ShapeDtypeStruct((nb, rows, KV_LORA), F32),
        compiler_params=_params(),
        name="decode_attention",
    )(page_table, qabs, qr, c_new, kr_new, cache_lat, cache_rope_t)


def _uv_kernel(olat_ref, wuv_ref, o_ref):
    n = olat_ref.shape[0] * SUBLANES
    for hd in range(N_HEADS):
        x = olat_ref[:, hd].reshape(n, KV_LORA).astype(BF16)
        o_ref[:, hd * V_HEAD:(hd + 1) * V_HEAD] = _mm(x, wuv_ref[hd]).astype(BF16)


def _uv_project(olat, wuv_heads):
    nb = olat.shape[0]
    return pl.pallas_call(
        _uv_kernel,
        grid=(1,),
        in_specs=[pl.BlockSpec(olat.shape, lambda i: (0, 0, 0, 0)),
                  pl.BlockSpec(wuv_heads.shape, lambda i: (0, 0, 0))],
        out_specs=pl.BlockSpec((nb * SUBLANES, N_HEADS * V_HEAD), lambda i: (0, 0)),
        out_shape=jax.ShapeDtypeStruct((nb * SUBLANES, N_HEADS * V_HEAD), BF16),
        compiler_params=_params(),
        name="uv_project",
    )(olat, wuv_heads)


def _mixer_out_kernel(x_ref, sh_ref, sc_ref, g1_ref, o_ref, cb_ref, wg_ref, woa_ref, woc_ref, wo_ref,
                      lng_ref, lnb_ref, x1_ref):
    x = x_ref[...]
    h = (x * (1.0 + sc_ref[...]) + sh_ref[...]).astype(BF16)
    gates = _mm(h, wg_ref[...])
    y_a = _mm(o_ref[...], woa_ref[...])
    y_c = _mm(cb_ref[...], woc_ref[...])
    mixed = _sigmoid(gates[:, :D_MODEL]) * y_a + _sigmoid(gates[:, D_MODEL:]) * y_c
    mix = _mm(mixed.astype(BF16), wo_ref[...])
    x1_ref[...] = _layer_norm(ALPHA * x + g1_ref[...] * mix, lng_ref[...], lnb_ref[...])


def _mixer_out(x2d, sh, sc, g1, o, cb, wts, *, prompt, tm, tiles_per_seq):
    ntok = x2d.shape[0]
    row = lambda w: pl.BlockSpec((tm, w), lambda i: (i, 0))
    ada_spec = (pl.BlockSpec((None, 1, D_MODEL), lambda i: (i // tiles_per_seq, 0, 0)) if prompt else row(D_MODEL))
    return pl.pallas_call(
        _mixer_out_kernel,
        grid=(ntok // tm,),
        in_specs=[row(D_MODEL), ada_spec, ada_spec, ada_spec, row(N_HEADS * V_HEAD), row(CONV_DIM)]
                 + [_const_spec(w.shape) for w in wts],
        out_specs=row(D_MODEL),
        out_shape=jax.ShapeDtypeStruct((ntok, D_MODEL), F32),
        compiler_params=_params(),
        name="mixer_out_prompt" if prompt else "mixer_out_sample",
    )(x2d, sh, sc, g1, o, cb, *wts)


def _ffn_kernel(x1_ref, sh_ref, sc_ref, g2_ref, w1_ref, w3_ref, w2_ref, lng_ref, lnb_ref, y_ref):
    x1 = x1_ref[...]
    h2 = (x1 * (1.0 + sc_ref[...]) + sh_ref[...]).astype(BF16)
    a = _mm(h2, w1_ref[...])
    gated = (a * _sigmoid(a)) * _mm(h2, w3_ref[...])
    f = _mm(gated.astype(BF16), w2_ref[...])
    y_ref[...] = _layer_norm(ALPHA * x1 + g2_ref[...] * f, lng_ref[...], lnb_ref[...])


def _ffn(x1, sh, sc, g2, wts, *, prompt, tm, tiles_per_seq):
    ntok = x1.shape[0]
    row = lambda w: pl.BlockSpec((tm, w), lambda i: (i, 0))
    ada_spec = (pl.BlockSpec((None, 1, D_MODEL), lambda i: (i // tiles_per_seq, 0, 0)) if prompt else row(D_MODEL))
    return pl.pallas_call(
        _ffn_kernel,
        grid=(ntok // tm,),
        in_specs=[row(D_MODEL), ada_spec, ada_spec, ada_spec] + [_const_spec(w.shape) for w in wts],
        out_specs=row(D_MODEL),
        out_shape=jax.ShapeDtypeStruct((ntok, D_MODEL), F32),
        compiler_params=_params(),
        name="ffn_prompt" if prompt else "ffn_sample",
    )(x1, sh, sc, g2, *wts)


def _rope_tables(pos, scale_q):
    inv = ROPE_THETA ** (-jnp.arange(0, QK_ROPE, 2, dtype=F32) / QK_ROPE)
    ang = pos.astype(F32)[:, None] * inv[None, :]
    cos, sin = jnp.cos(ang), jnp.sin(ang)
    n = pos.shape[0]
    z_lo = jnp.zeros((n, ROPE_LO), F32)
    z_hi = jnp.zeros((n, LANES - ROPE_LO - QK_ROPE), F32)
    cos_k = jnp.concatenate([z_lo, cos, cos, z_hi], axis=1)
    sin_k = jnp.concatenate([z_lo, -sin, sin, z_hi], axis=1)
    cos_q = jnp.concatenate([jnp.ones((n, ROPE_LO), F32), cos, cos, z_hi], axis=1) * scale_q
    sin_q = sin_k * scale_q
    return cos_q, sin_q, cos_k, sin_k


def _head_groups(w, width):
    k = w.shape[0]
    return jnp.pad(w, ((0, 0), (0, 0), (0, HEAD_SLAB - width))).reshape(k, N_HEADS * HEAD_SLAB)


def kernel(x_prompt, x_sample, cache_kv_latent, cache_k_rope, state_conv, page_table, c_prompt, c_sample,
           w_ada, b_ada, w_in, q_norm_g, kv_norm_g, w_uq, w_ukv, w_oa, conv_w, w_oc, w_o,
           ln1_g, ln1_b, w_ff1, w_ff3, w_ff2, ln2_g, ln2_b):
    assert w_ada.shape[0] == DEPTH == 1
    nb_p, seq_p, _ = x_prompt.shape
    nb_s, seq_s, _ = x_sample.shape
    assert seq_s == SUBLANES
    n_pool, page = cache_kv_latent.shape[1], cache_kv_latent.shape[2]
    past_len = page_table.shape[1] * page
    layer = 0

    s0, s1, s2, s3 = Q_LORA, Q_LORA + KV_LORA, Q_LORA + KV_LORA + QK_ROPE, Q_LORA + KV_LORA + QK_ROPE + 3 * CONV_DIM
    w_in_l = w_in[layer]
    w_kr_group = jnp.pad(w_in_l[:, s1:s2], ((0, 0), (ROPE_LO, LANES - ROPE_LO - QK_ROPE)))
    wa = jnp.concatenate([w_in_l[:, :s1], w_kr_group], axis=1).astype(BF16)
    wb = w_in_l[:, s2:s3].astype(BF16)
    wg = w_in_l[:, s3:].astype(BF16)
    wuq = _head_groups(w_uq[layer].reshape(Q_LORA, N_HEADS, QK_HEAD), QK_HEAD).astype(BF16)
    w_uk = w_ukv[layer][:, :, :QK_NOPE]
    w_uv = w_ukv[layer][:, :, QK_NOPE:]
    wuk_groups = _head_groups(w_uk, QK_NOPE).astype(BF16)
    wuv_flat = w_uv.reshape(KV_LORA, N_HEADS * V_HEAD).astype(BF16)
    wuv_heads = w_uv.transpose(1, 0, 2).astype(BF16)
    wukt = jnp.pad(w_uk.transpose(1, 2, 0), ((0, 0), (0, HEAD_SLAB - QK_NOPE), (0, 0))).astype(BF16)
    qg = q_norm_g[layer][None, :]
    kvg = kv_norm_g[layer][None, :]
    convw = conv_w[layer]
    in_wts = (wa, wb, qg, kvg, wuq, convw)
    out_wts = (wg, w_oa[layer].astype(BF16), w_oc[layer].astype(BF16), w_o[layer].astype(BF16),
               ln1_g[layer][None, :], ln1_b[layer][None, :])
    ffn_wts = (w_ff1[layer].astype(BF16), w_ff3[layer].astype(BF16), w_ff2[layer].astype(BF16),
               ln2_g[layer][None, :], ln2_b[layer][None, :])

    ada = _ada_terms(jnp.concatenate([c_prompt, c_sample], axis=0), w_ada[layer].astype(BF16), b_ada[layer][None, :])
    ada_p = [a[:, None, :] for a in jnp.split(ada[:nb_p], 6, axis=-1)]
    ada_s = [jnp.repeat(a, seq_s, axis=0) for a in jnp.split(ada[nb_p:], 6, axis=-1)]

    tm_p = 512
    tps = seq_p // tm_p
    xp = x_prompt.reshape(nb_p * seq_p, D_MODEL)
    tabs_p = _rope_tables(jnp.arange(seq_p), ATTN_SCALE)
    ckv_p, kr_p, cb_p, u_p, q_p, k_p, v_p = _mixer_in(
        xp, ada_p[0], ada_p[1], in_wts, tabs_p, prompt=True, nb=nb_p, seq=seq_p, tm=tm_p,
        extra=(wuk_groups, wuv_flat))
    o_p = _prompt_attention(q_p, k_p, v_p.reshape(nb_p, seq_p, N_HEADS * V_HEAD), nb=nb_p, seq=seq_p)
    x1_p = _mixer_out(xp, ada_p[0], ada_p[1], ada_p[2], o_p.reshape(nb_p * seq_p, N_HEADS * V_HEAD), cb_p, out_wts,
                      prompt=True, tm=tm_p, tiles_per_seq=tps)
    y_p = _ffn(x1_p, ada_p[3], ada_p[4], ada_p[5], ffn_wts, prompt=True, tm=tm_p, tiles_per_seq=tps)

    tm_s = 256
    ntok_s = nb_s * seq_s
    xs = x_sample.reshape(ntok_s, D_MODEL)
    tabs_s = tuple(jnp.tile(t, (tm_s // seq_s, 1)) for t in _rope_tables(past_len + jnp.arange(seq_s), ATTN_SCALE))
    st = state_conv[layer]
    zrow = jnp.zeros((nb_s, seq_s - 1, CONV_DIM), F32)
    fix1 = jnp.concatenate([st[:, 1:2], zrow], axis=1).reshape(ntok_s, CONV_DIM)
    fix2 = jnp.concatenate([st, zrow[:, 1:]], axis=1).reshape(ntok_s, CONV_DIM)
    ckv_s, kr_s, cb_s, u_s, qabs_s, qr_s = _mixer_in(
        xs, ada_s[0], ada_s[1], in_wts, tabs_s, prompt=False, nb=nb_s, seq=seq_s, tm=tm_s,
        extra=(wukt, fix1, fix2))
    rows = N_HEADS * seq_s
    olat = _decode_attention(
        page_table, qabs_s.reshape(nb_s, rows, KV_LORA), qr_s.reshape(nb_s, rows, QK_ROPE),
        ckv_s.reshape(nb_s, seq_s, KV_LORA), kr_s.reshape(nb_s, seq_s, QK_ROPE),
        cache_kv_latent.reshape(n_pool, page, KV_LORA),
        jnp.swapaxes(cache_k_rope.reshape(n_pool, page, QK_ROPE), 1, 2))
    o_s = _uv_project(olat.reshape(nb_s, N_HEADS, seq_s, KV_LORA), wuv_heads)
    x1_s = _mixer_out(xs, ada_s[0], ada_s[1], ada_s[2], o_s, cb_s, out_wts, prompt=False, tm=tm_s, tiles_per_seq=1)
    y_s = _ffn(x1_s, ada_s[3], ada_s[4], ada_s[5], ffn_wts, prompt=False, tm=tm_s, tiles_per_seq=1)

    hist = CONV_WIDTH - 1
    return (y_p.reshape(nb_p, seq_p, D_MODEL),
            y_s.reshape(nb_s, seq_s, D_MODEL),
            ckv_p.reshape(1, nb_p, seq_p, KV_LORA),
            kr_p.reshape(1, nb_p, seq_p, QK_ROPE),
            u_p.reshape(1, nb_p, SUBLANES, CONV_DIM)[:, :, SUBLANES - hist:],
            ckv_s.reshape(1, nb_s, seq_s, KV_LORA),
            kr_s.reshape(1, nb_s, seq_s, QK_ROPE),
            u_s.reshape(1, nb_s, seq_s, CONV_DIM)[:, :, seq_s - hist:])
```

```python
import functools
import math

import jax
import jax.numpy as jnp
from jax import lax
from jax.experimental import pallas as pl
from jax.experimental.pallas import tpu as pltpu

D_MODEL = 1024
N_HEADS = 8
QK_NOPE = 64
QK_ROPE = 32
QK_HEAD = QK_NOPE + QK_ROPE
V_HEAD = 64
Q_LORA = 384
KV_LORA = 256
ROPE_THETA = 10000.0
ATTN_SCALE = QK_HEAD ** -0.5
CONV_DIM = 512
CONV_WIDTH = 3
D_FF = int(math.ceil(8 * D_MODEL / 3 / 256)) * 256
DEPTH = 1
ALPHA = (2.0 * DEPTH) ** 0.25
LN_EPS = 1e-5
RMS_EPS = 1e-6

LANES = 128
SUBLANES = 8
HEAD_SLAB = LANES
ROPE_LO = QK_NOPE
ROPE_HALF = QK_ROPE // 2
VMEM_LIMIT = 52 * 1024 * 1024

BF16 = jnp.bfloat16
F32 = jnp.float32
NT_DIMS = (((1,), (1,)), ((), ()))


def _params(n_axes=1, vmem=VMEM_LIMIT):
    return pltpu.CompilerParams(dimension_semantics=("arbitrary",) * n_axes, vmem_limit_bytes=vmem)


def _const_spec(shape):
    nd = len(shape)
    return pl.BlockSpec(shape, lambda *_: (0,) * nd, pipeline_mode=pl.Buffered(1))


def _mm(a, b):
    return jnp.dot(a, b, preferred_element_type=F32)


def _sigmoid(x):
    return 1.0 / (1.0 + jnp.exp(-x))


def _layer_norm(x, g, b):
    mu = jnp.mean(x, axis=-1, keepdims=True)
    xc = x - mu
    var = jnp.mean(xc * xc, axis=-1, keepdims=True)
    return xc * lax.rsqrt(var + LN_EPS) * g + b


def _rms_norm(x, g):
    return x * lax.rsqrt(jnp.mean(x * x, axis=-1, keepdims=True) + RMS_EPS) * g


def _rows(ref, r, tm):
    return ref[r, :] if ref.shape[0] == tm else ref[...]


def _pipelined(tm, sub, front, back):
    subs = [slice(r0, r0 + sub) for r0 in range(0, tm, sub)]
    pending = front(subs[0])
    for n, r in enumerate(subs):
        nxt = front(subs[n + 1]) if n + 1 < len(subs) else None
        back(r, *pending)
        pending = nxt


def _rope_slab(x, cos_t, sin_t):
    lane = lax.broadcasted_iota(jnp.int32, x.shape, 1)
    swapped = jnp.where(lane < ROPE_LO + ROPE_HALF,
                        pltpu.roll(x, LANES - ROPE_HALF, 1), pltpu.roll(x, ROPE_HALF, 1))
    return x * cos_t + swapped * sin_t


def _ada_kernel(c_ref, w_ref, b_ref, o_ref):
    c = c_ref[...]
    s = c * _sigmoid(c)
    o_ref[...] = _mm(s.astype(BF16), w_ref[...]) + b_ref[...]


def _ada_terms(c, w_ada, b_ada):
    n, tn = c.shape[0], 1536
    return pl.pallas_call(
        _ada_kernel,
        grid=(6 * D_MODEL // tn,),
        in_specs=[pl.BlockSpec((n, D_MODEL), lambda j: (0, 0)),
                  pl.BlockSpec((D_MODEL, tn), lambda j: (0, j)),
                  pl.BlockSpec((1, tn), lambda j: (0, j))],
        out_specs=pl.BlockSpec((n, tn), lambda j: (0, j)),
        out_shape=jax.ShapeDtypeStruct((n, 6 * D_MODEL), F32),
        compiler_params=_params(),
        name="ada_terms",
    )(c, w_ada, b_ada)


def _mixer_in_kernel(*refs, prompt, tm, sub, tiles_per_seq):
    if prompt:
        (x_ref, sh_ref, sc_ref, wa_ref, wb_ref, qg_ref, kvg_ref, wuq_ref, convw_ref,
         cq_ref, sq_ref, ck_ref, sk_ref, wuk_ref, wuvt_ref,
         ckv_ref, kr_ref, cb_ref, u_ref, q_ref, k_ref, vt_ref, ubuf) = refs
    else:
        (x_ref, sh_ref, sc_ref, wa_ref, wb_ref, qg_ref, kvg_ref, wuq_ref, convw_ref,
         cq_ref, sq_ref, ck_ref, sk_ref, wukt_ref, fix1_ref, fix2_ref,
         ckv_ref, kr_ref, cb_ref, u_ref, qabs_ref, qr_ref, ubuf) = refs
    i = pl.program_id(0)

    @pl.when(i % tiles_per_seq == 0)
    def _():
        ubuf[0:SUBLANES, :] = jnp.zeros((SUBLANES, CONV_DIM), F32)

    def front(r):
        h = (x_ref[r, :] * (1.0 + _rows(sc_ref, r, tm)) + _rows(sh_ref, r, tm)).astype(BF16)
        return _mm(h, wa_ref[...]), _mm(h, wb_ref[...])

    def back(r, za, zb):
        r0, n = r.start, r.stop - r.start
        q_lat = za[:, :Q_LORA]
        kv_lat = za[:, Q_LORA:Q_LORA + KV_LORA]
        kr_slab = za[:, Q_LORA + KV_LORA:]
        c_kv = _rms_norm(kv_lat, kvg_ref[...])
        ckv_ref[r, :] = c_kv
        kr_rot = _rope_slab(kr_slab, ck_ref[r, :], sk_ref[r, :])
        kr_ref[r, :] = kr_rot[:, ROPE_LO:ROPE_LO + QK_ROPE]

        q_all = _mm(_rms_norm(q_lat, qg_ref[...]).astype(BF16), wuq_ref[...])
        cq, sq = cq_ref[r, :], sq_ref[r, :]
        q_heads = [_rope_slab(q_all[:, hd * HEAD_SLAB:(hd + 1) * HEAD_SLAB], cq, sq) for hd in range(N_HEADS)]

        b_g = zb[:, :CONV_DIM]
        u = zb[:, CONV_DIM:2 * CONV_DIM] * zb[:, 2 * CONV_DIM:]
        ubuf[SUBLANES + r0:SUBLANES + r0 + n, :] = u
        u1 = ubuf[pl.ds(SUBLANES + r0 - 1, n), :]
        u2 = ubuf[pl.ds(SUBLANES + r0 - 2, n), :]
        if prompt:
            if r.stop == tm:
                ubuf[0:SUBLANES, :] = u[n - SUBLANES:, :]
                u_ref[...] = u[n - SUBLANES:, :]
        else:
            t = lax.broadcasted_iota(jnp.int32, (n, CONV_DIM), 0) % SUBLANES
            u1 = jnp.where(t == 0, fix1_ref[r, :], u1)
            u2 = jnp.where(t < 2, fix2_ref[r, :], u2)
            u_ref[r, :] = u
        cw = convw_ref[...]
        conv_y = u2 * cw[0:1, :] + u1 * cw[1:2, :] + u * cw[2:3, :]
        cb_ref[r, :] = (b_g * conv_y).astype(BF16)

        if prompt:
            c_bf = c_kv.astype(BF16)
            k_nope = _mm(c_bf, wuk_ref[...])
            vt_ref[:, r] = lax.dot_general(wuvt_ref[...], c_bf, NT_DIMS, preferred_element_type=F32).astype(BF16)
            for hd in range(N_HEADS):
                q_ref[hd, r, :] = q_heads[hd].astype(BF16)
                k_ref[hd, r, :] = (k_nope[:, hd * HEAD_SLAB:(hd + 1) * HEAD_SLAB] + kr_rot).astype(BF16)
        else:
            seqs = slice(r0 // SUBLANES, r.stop // SUBLANES)
            for hd in range(N_HEADS):
                qh = q_heads[hd]
                qabs = _mm(qh.astype(BF16), wukt_ref[hd])
                qabs_ref[seqs, hd] = qabs.reshape(n // SUBLANES, SUBLANES, KV_LORA)
                qr_ref[seqs, hd] = qh[:, ROPE_LO:ROPE_LO + QK_ROPE].reshape(n // SUBLANES, SUBLANES, QK_ROPE)

    _pipelined(tm, sub, front, back)


def _mixer_in(x2d, sh, sc, wts, tabs, *, prompt, nb, seq, tm, sub, extra):
    ntok = nb * seq
    tiles_per_seq = seq // tm if prompt else 1
    grid = (ntok // tm,)
    row = lambda w: pl.BlockSpec((tm, w), lambda i: (i, 0))
    if prompt:
        ada_spec = pl.BlockSpec((None, 1, D_MODEL), lambda i: (i // tiles_per_seq, 0, 0))
        tab_spec = pl.BlockSpec((tm, LANES), lambda i: (i % tiles_per_seq, 0))
    else:
        ada_spec = row(D_MODEL)
        tab_spec = pl.BlockSpec((tm, LANES), lambda i: (0, 0))
    wa, wb, qg, kvg, wuq, convw = wts
    in_specs = [row(D_MODEL), ada_spec, ada_spec,
                _const_spec(wa.shape), _const_spec(wb.shape), _const_spec(qg.shape), _const_spec(kvg.shape),
                _const_spec(wuq.shape), _const_spec(convw.shape),
                tab_spec, tab_spec, tab_spec, tab_spec]
    out_shape = [jax.ShapeDtypeStruct((ntok, KV_LORA), F32), jax.ShapeDtypeStruct((ntok, QK_ROPE), F32),
                 jax.ShapeDtypeStruct((ntok, CONV_DIM), BF16)]
    out_specs = [row(KV_LORA), row(QK_ROPE), row(CONV_DIM)]
    if prompt:
        wuk, wuv = extra
        in_specs += [_const_spec(wuk.shape), _const_spec(wuv.shape)]
        head_spec = pl.BlockSpec((None, N_HEADS, tm, HEAD_SLAB),
                                 lambda i: (i // tiles_per_seq, 0, i % tiles_per_seq, 0))
        out_shape += [jax.ShapeDtypeStruct((nb * SUBLANES, CONV_DIM), F32),
                      jax.ShapeDtypeStruct((nb, N_HEADS, seq, HEAD_SLAB), BF16),
                      jax.ShapeDtypeStruct((nb, N_HEADS, seq, HEAD_SLAB), BF16),
                      jax.ShapeDtypeStruct((nb, N_HEADS * V_HEAD, seq), BF16)]
        out_specs += [pl.BlockSpec((SUBLANES, CONV_DIM), lambda i: (i // tiles_per_seq, 0)),
                      head_spec, head_spec,
                      pl.BlockSpec((None, N_HEADS * V_HEAD, tm), lambda i: (i // tiles_per_seq, 0, i % tiles_per_seq))]
    else:
        wukt, fix1, fix2 = extra
        in_specs += [_const_spec(wukt.shape), row(CONV_DIM), row(CONV_DIM)]
        nseq = tm // SUBLANES
        out_shape += [jax.ShapeDtypeStruct((ntok, CONV_DIM), F32),
                      jax.ShapeDtypeStruct((nb, N_HEADS, SUBLANES, KV_LORA), F32),
                      jax.ShapeDtypeStruct((nb, N_HEADS, SUBLANES, QK_ROPE), F32)]
        out_specs += [row(CONV_DIM),
                      pl.BlockSpec((nseq, N_HEADS, SUBLANES, KV_LORA), lambda i: (i, 0, 0, 0)),
                      pl.BlockSpec((nseq, N_HEADS, SUBLANES, QK_ROPE), lambda i: (i, 0, 0, 0))]
    return pl.pallas_call(
        functools.partial(_mixer_in_kernel, prompt=prompt, tm=tm, sub=sub, tiles_per_seq=tiles_per_seq),
        grid=grid, in_specs=in_specs, out_specs=out_specs, out_shape=out_shape,
        scratch_shapes=[pltpu.VMEM((SUBLANES + tm, CONV_DIM), F32)],
        compiler_params=_params(),
        name="mixer_in_prompt" if prompt else "mixer_in_sample",
    )(x2d, sh, sc, wa, wb, qg, kvg, wuq, convw, *tabs, *extra)


def _prompt_attn_kernel(q_ref, k_ref, vt_ref, o_ref, *, seq, tq):
    key_le_query = (lax.broadcasted_iota(jnp.int32, (tq, tq), 0) <= lax.broadcasted_iota(jnp.int32, (tq, tq), 1))

    def scores(i, hh):
        lo, hi = i * tq, (i + 1) * tq
        q = q_ref[hh, lo:hi, :]
        sd = lax.dot_general(k_ref[hh, lo:hi, :], q, NT_DIMS, preferred_element_type=F32)
        sd = jnp.where(key_le_query, sd, -jnp.inf)
        so = lax.dot_general(k_ref[hh, 0:lo, :], q, NT_DIMS, preferred_element_type=F32) if i > 0 else None
        return sd, so

    def softmax_pv(i, hh, sd, so):
        lo, hi = i * tq, (i + 1) * tq
        vt = vt_ref.at[hh * V_HEAD:(hh + 1) * V_HEAD, :]
        m = jnp.max(sd, axis=0, keepdims=True)
        if so is not None:
            m = jnp.maximum(m, jnp.max(so, axis=0, keepdims=True))
        pd = jnp.exp2(sd - m)
        l = jnp.sum(pd, axis=0, keepdims=True)
        acc = _mm(vt[:, lo:hi], pd.astype(BF16))
        if so is not None:
            po = jnp.exp2(so - m)
            l = l + jnp.sum(po, axis=0, keepdims=True)
            acc = acc + _mm(vt[:, 0:lo], po.astype(BF16))
        return acc * (1.0 / l)

    n_tiles = seq // tq
    units = [(i, hh) for i in reversed(range(n_tiles)) for hh in range(2)]
    pending = scores(*units[0])
    outs = []
    for n, (i, hh) in enumerate(units):
        nxt = scores(*units[n + 1]) if n + 1 < len(units) else None
        outs.append(softmax_pv(i, hh, *pending))
        pending = nxt
        if hh == 1:
            o_ref[i * tq:(i + 1) * tq, :] = jnp.concatenate(outs, axis=0).T.astype(BF16)
            outs = []


def _prompt_attention(q, k, vt, *, nb, seq, tq=512):
    head_spec = pl.BlockSpec((None, 2, seq, HEAD_SLAB), lambda b, p: (b, p, 0, 0))
    return pl.pallas_call(
        functools.partial(_prompt_attn_kernel, seq=seq, tq=tq),
        grid=(nb, N_HEADS // 2),
        in_specs=[head_spec, head_spec, pl.BlockSpec((None, 2 * V_HEAD, seq), lambda b, p: (b, p, 0))],
        out_specs=pl.BlockSpec((None, seq, 2 * V_HEAD), lambda b, p: (b, 0, p)),
        out_shape=jax.ShapeDtypeStruct((nb, seq, N_HEADS * V_HEAD), BF16),
        compiler_params=_params(2),
        name="prompt_attention",
    )(q, k, vt)


def _decode_attn_kernel(pt_ref, qabs_ref, qr_ref, cnew_ref, krnew_ref, lat_hbm, ropet_hbm, olat_ref,
                        latbuf, ropebuf, sems, s_all, s_new_buf, *group_bufs,
                        n_pages, group, page, n_slots, n_seq):
    b = pl.program_id(0)
    n_groups = n_pages // group
    gk = group * page
    rows = N_HEADS * SUBLANES
    cbf, krbf = group_bufs[:n_groups], group_bufs[n_groups:]

    def page_copies(seq_idx, g, slot):
        out = []
        for p in range(group):
            pid = pt_ref[seq_idx, g * group + p]
            out.append(pltpu.make_async_copy(lat_hbm.at[pid], latbuf.at[slot, pl.ds(p * page, page)], sems.at[0, slot]))
            out.append(pltpu.make_async_copy(ropet_hbm.at[pid], ropebuf.at[slot, :, pl.ds(p * page, page)], sems.at[1, slot]))
        return out

    @pl.when(b == 0)
    def _():
        for g in range(n_slots):
            for cp in page_copies(0, g, g):
                cp.start()

    def new_token_block(ref, width):
        return jnp.concatenate([ref[0], jnp.zeros((SUBLANES, width), F32)], axis=0).astype(BF16)

    def stage(g):
        slot = g % n_slots
        for cp in page_copies(b, g, slot):
            cp.wait()
        cbf[g][...] = latbuf[slot].astype(BF16)
        krbf[g][...] = ropebuf[slot].astype(BF16)

    def refill(g):
        nxt = g + n_slots
        if nxt < n_groups:
            for cp in page_copies(b, nxt, g % n_slots):
                cp.start()
        else:
            @pl.when(b + 1 < n_seq)
            def _():
                for cp in page_copies(b + 1, nxt - n_groups, g % n_slots):
                    cp.start()

    def phase1():
        qa = qabs_ref[0].astype(BF16)
        qr = qr_ref[0].astype(BF16)
        s_new = (lax.dot_general(qa, new_token_block(cnew_ref, KV_LORA), NT_DIMS, preferred_element_type=F32)
                 + lax.dot_general(qr, new_token_block(krnew_ref, QK_ROPE), NT_DIMS, preferred_element_type=F32))
        t_row = lax.broadcasted_iota(jnp.int32, (rows, 2 * SUBLANES), 0) % SUBLANES
        k_col = lax.broadcasted_iota(jnp.int32, (rows, 2 * SUBLANES), 1)
        s_new_buf[...] = jnp.where(k_col <= t_row, s_new, -jnp.inf)
        stage(0)
        for g in range(n_groups):
            if g + 1 < n_groups:
                stage(g + 1)
            s_all[:, g * gk:(g + 1) * gk] = (lax.dot_general(qa, cbf[g][...], NT_DIMS, preferred_element_type=F32)
                                             + _mm(qr, krbf[g][...]))
            refill(g)

    def phase2():
        cn = new_token_block(cnew_ref, KV_LORA)
        sn = s_new_buf[...]
        m = jnp.maximum(jnp.max(sn, axis=-1, keepdims=True), jnp.max(s_all[...], axis=-1, keepdims=True))
        p = jnp.exp(sn - m)
        l = jnp.sum(p, axis=-1, keepdims=True)
        acc = _mm(p.astype(BF16), cn)
        for g in range(n_groups):
            p = jnp.exp(s_all[:, g * gk:(g + 1) * gk] - m)
            l = l + jnp.sum(p, axis=-1, keepdims=True)
            acc = acc + _mm(p.astype(BF16), cbf[g][...])
        olat_ref[0] = acc * (1.0 / l)

    pl.when(b >= 0)(phase1)
    pl.when(b < n_seq)(phase2)


def _decode_attention(page_table, qabs, qr, c_new, kr_new, cache_lat, cache_rope_t, *, group=16, n_slots=8):
    nb, n_pages = page_table.shape
    page = cache_lat.shape[1]
    rows = N_HEADS * SUBLANES
    n_groups = n_pages // group
    assert n_pages % group == 0 and n_groups % n_slots == 0
    per_seq = lambda w, r: pl.BlockSpec((1, r, w), lambda b, pt: (b, 0, 0))
    grid_spec = pltpu.PrefetchScalarGridSpec(
        num_scalar_prefetch=1,
        grid=(nb,),
        in_specs=[per_seq(KV_LORA, rows), per_seq(QK_ROPE, rows), per_seq(KV_LORA, SUBLANES), per_seq(QK_ROPE, SUBLANES),
                  pl.BlockSpec(memory_space=pl.ANY), pl.BlockSpec(memory_space=pl.ANY)],
        out_specs=per_seq(KV_LORA, rows),
        scratch_shapes=[pltpu.VMEM((n_slots, group * page, KV_LORA), F32),
                        pltpu.VMEM((n_slots, QK_ROPE, group * page), F32),
                        pltpu.SemaphoreType.DMA((2, n_slots)),
                        pltpu.VMEM((rows, n_pages * page), F32),
                        pltpu.VMEM((rows, 2 * SUBLANES), F32)]
                       + [pltpu.VMEM((group * page, KV_LORA), BF16)] * n_groups
                       + [pltpu.VMEM((QK_ROPE, group * page), BF16)] * n_groups,
    )
    return pl.pallas_call(
        functools.partial(_decode_attn_kernel, n_pages=n_pages, group=group, page=page, n_slots=n_slots, n_seq=nb),
        grid_spec=grid_spec,
        out_shape=jax.ShapeDtypeStruct((nb, rows, KV_LORA), F32),
        compiler_params=_params(),
        name="decode_attention",
    )(page_table, qabs, qr, c_new, kr_new, cache_lat, cache_rope_t)


def _uv_kernel(olat_ref, wuv_ref, o_ref):
    n = olat_ref.shape[0] * SUBLANES
    for hd in range(N_HEADS):
        x = olat_ref[:, hd].reshape(n, KV_LORA).astype(BF16)
        o_ref[:, hd * V_HEAD:(hd + 1) * V_HEAD] = _mm(x, wuv_ref[hd]).astype(BF16)


def _uv_project(olat, wuv_heads):
    nb = olat.shape[0]
    return pl.pallas_call(
        _uv_kernel,
        grid=(1,),
        in_specs=[pl.BlockSpec(olat.shape, lambda i: (0, 0, 0, 0)),
                  pl.BlockSpec(wuv_heads.shape, lambda i: (0, 0, 0))],
        out_specs=pl.BlockSpec((nb * SUBLANES, N_HEADS * V_HEAD), lambda i: (0, 0)),
        out_shape=jax.ShapeDtypeStruct((nb * SUBLANES, N_HEADS * V_HEAD), BF16),
        compiler_params=_params(),
        name="uv_project",
    )(olat, wuv_heads)


def _mixer_out_kernel(x_ref, sh_ref, sc_ref, g1_ref, o_ref, cb_ref, wg_ref, woa_ref, woc_ref, wo_ref,
                      lng_ref, lnb_ref, x1_ref, *, tm, sub):
    def front(r):
        h = (x_ref[r, :] * (1.0 + _rows(sc_ref, r, tm)) + _rows(sh_ref, r, tm)).astype(BF16)
        return _mm(h, wg_ref[...]), _mm(o_ref[r, :], woa_ref[...]), _mm(cb_ref[r, :], woc_ref[...])

    def back(r, gates, y_a, y_c):
        mixed = _sigmoid(gates[:, :D_MODEL]) * y_a + _sigmoid(gates[:, D_MODEL:]) * y_c
        mix = _mm(mixed.astype(BF16), wo_ref[...])
        x1_ref[r, :] = _layer_norm(ALPHA * x_ref[r, :] + _rows(g1_ref, r, tm) * mix, lng_ref[...], lnb_ref[...])

    _pipelined(tm, sub, front, back)


def _mixer_out(x2d, sh, sc, g1, o, cb, wts, *, prompt, tm, sub, tiles_per_seq):
    ntok = x2d.shape[0]
    row = lambda w: pl.BlockSpec((tm, w), lambda i: (i, 0))
    ada_spec = (pl.BlockSpec((None, 1, D_MODEL), lambda i: (i // tiles_per_seq, 0, 0)) if prompt else row(D_MODEL))
    return pl.pallas_call(
        functools.partial(_mixer_out_kernel, tm=tm, sub=sub),
        grid=(ntok // tm,),
        in_specs=[row(D_MODEL), ada_spec, ada_spec, ada_spec, row(N_HEADS * V_HEAD), row(CONV_DIM)]
                 + [_const_spec(w.shape) for w in wts],
        out_specs=row(D_MODEL),
        out_shape=jax.ShapeDtypeStruct((ntok, D_MODEL), F32),
        compiler_params=_params(),
        name="mixer_out_prompt" if prompt else "mixer_out_sample",
    )(x2d, sh, sc, g1, o, cb, *wts)


def _ffn_kernel(x1_ref, sh_ref, sc_ref, g2_ref, w1_ref, w3_ref, w2_ref, lng_ref, lnb_ref, y_ref, *, tm, sub):
    def front(r):
        h2 = (x1_ref[r, :] * (1.0 + _rows(sc_ref, r, tm)) + _rows(sh_ref, r, tm)).astype(BF16)
        return _mm(h2, w1_ref[...]), _mm(h2, w3_ref[...])

    def back(r, a, g):
        gated = (a * _sigmoid(a)) * g
        f = _mm(gated.astype(BF16), w2_ref[...])
        y_ref[r, :] = _layer_norm(ALPHA * x1_ref[r, :] + _rows(g2_ref, r, tm) * f, lng_ref[...], lnb_ref[...])

    _pipelined(tm, sub, front, back)


def _ffn(x1, sh, sc, g2, wts, *, prompt, tm, sub, tiles_per_seq):
    ntok = x1.shape[0]
    row = lambda w: pl.BlockSpec((tm, w), lambda i: (i, 0))
    ada_spec = (pl.BlockSpec((None, 1, D_MODEL), lambda i: (i // tiles_per_seq, 0, 0)) if prompt else row(D_MODEL))
    return pl.pallas_call(
        functools.partial(_ffn_kernel, tm=tm, sub=sub),
        grid=(ntok // tm,),
        in_specs=[row(D_MODEL), ada_spec, ada_spec, ada_spec] + [_const_spec(w.shape) for w in wts],
        out_specs=row(D_MODEL),
        out_shape=jax.ShapeDtypeStruct((ntok, D_MODEL), F32),
        compiler_params=_params(),
        name="ffn_prompt" if prompt else "ffn_sample",
    )(x1, sh, sc, g2, *wts)


def _rope_tables(pos, scale_q):
    inv = ROPE_THETA ** (-jnp.arange(0, QK_ROPE, 2, dtype=F32) / QK_ROPE)
    ang = pos.astype(F32)[:, None] * inv[None, :]
    cos, sin = jnp.cos(ang), jnp.sin(ang)
    n = pos.shape[0]
    z_lo = jnp.zeros((n, ROPE_LO), F32)
    z_hi = jnp.zeros((n, LANES - ROPE_LO - QK_ROPE), F32)
    cos_k = jnp.concatenate([z_lo, cos, cos, z_hi], axis=1)
    sin_k = jnp.concatenate([z_lo, -sin, sin, z_hi], axis=1)
    cos_q = jnp.concatenate([jnp.ones((n, ROPE_LO), F32), cos, cos, z_hi], axis=1) * scale_q
    sin_q = sin_k * scale_q
    return cos_q, sin_q, cos_k, sin_k


def _head_groups(w, width):
    k = w.shape[0]
    return jnp.pad(w, ((0, 0), (0, 0), (0, HEAD_SLAB - width))).reshape(k, N_HEADS * HEAD_SLAB)


def kernel(x_prompt, x_sample, cache_kv_latent, cache_k_rope, state_conv, page_table, c_prompt, c_sample,
           w_ada, b_ada, w_in, q_norm_g, kv_norm_g, w_uq, w_ukv, w_oa, conv_w, w_oc, w_o,
           ln1_g, ln1_b, w_ff1, w_ff3, w_ff2, ln2_g, ln2_b):
    assert w_ada.shape[0] == DEPTH == 1
    nb_p, seq_p, _ = x_prompt.shape
    nb_s, seq_s, _ = x_sample.shape
    assert seq_s == SUBLANES
    n_pool, page = cache_kv_latent.shape[1], cache_kv_latent.shape[2]
    past_len = page_table.shape[1] * page
    layer = 0

    s0, s1, s2, s3 = Q_LORA, Q_LORA + KV_LORA, Q_LORA + KV_LORA + QK_ROPE, Q_LORA + KV_LORA + QK_ROPE + 3 * CONV_DIM
    w_in_l = w_in[layer]
    w_kr_group = jnp.pad(w_in_l[:, s1:s2], ((0, 0), (ROPE_LO, LANES - ROPE_LO - QK_ROPE)))
    wa = jnp.concatenate([w_in_l[:, :s1], w_kr_group], axis=1).astype(BF16)
    wb = w_in_l[:, s2:s3].astype(BF16)
    wg = w_in_l[:, s3:].astype(BF16)
    wuq = _head_groups(w_uq[layer].reshape(Q_LORA, N_HEADS, QK_HEAD), QK_HEAD).astype(BF16)
    w_uk = w_ukv[layer][:, :, :QK_NOPE]
    w_uv = w_ukv[layer][:, :, QK_NOPE:]
    wuk_groups = _head_groups(w_uk, QK_NOPE).astype(BF16)
    wuv_t = w_uv.reshape(KV_LORA, N_HEADS * V_HEAD).T.astype(BF16)
    wuv_heads = w_uv.transpose(1, 0, 2).astype(BF16)
    wukt = jnp.pad(w_uk.transpose(1, 2, 0), ((0, 0), (0, HEAD_SLAB - QK_NOPE), (0, 0))).astype(BF16)
    qg = q_norm_g[layer][None, :]
    kvg = kv_norm_g[layer][None, :]
    convw = conv_w[layer]
    in_wts = (wa, wb, qg, kvg, wuq, convw)
    out_wts = (wg, w_oa[layer].astype(BF16), w_oc[layer].astype(BF16), w_o[layer].astype(BF16),
               ln1_g[layer][None, :], ln1_b[layer][None, :])
    ffn_wts = (w_ff1[layer].astype(BF16), w_ff3[layer].astype(BF16), w_ff2[layer].astype(BF16),
               ln2_g[layer][None, :], ln2_b[layer][None, :])

    ada = _ada_terms(jnp.concatenate([c_prompt, c_sample], axis=0), w_ada[layer].astype(BF16), b_ada[layer][None, :])
    ada_p = [a[:, None, :] for a in jnp.split(ada[:nb_p], 6, axis=-1)]
    ada_s = [jnp.repeat(a, seq_s, axis=0) for a in jnp.split(ada[nb_p:], 6, axis=-1)]

    tm_p = 1024
    tps = seq_p // tm_p
    xp = x_prompt.reshape(nb_p * seq_p, D_MODEL)
    tabs_p = _rope_tables(jnp.arange(seq_p), ATTN_SCALE * math.log2(math.e))
    ckv_p, kr_p, cb_p, u_p, q_p, k_p, vt_p = _mixer_in(
        xp, ada_p[0], ada_p[1], in_wts, tabs_p, prompt=True, nb=nb_p, seq=seq_p, tm=1024, sub=256,
        extra=(wuk_groups, wuv_t))
    o_p = _prompt_attention(q_p, k_p, vt_p, nb=nb_p, seq=seq_p)
    x1_p = _mixer_out(xp, ada_p[0], ada_p[1], ada_p[2], o_p.reshape(nb_p * seq_p, N_HEADS * V_HEAD), cb_p, out_wts,
                      prompt=True, tm=tm_p, sub=256, tiles_per_seq=tps)
    y_p = _ffn(x1_p, ada_p[3], ada_p[4], ada_p[5], ffn_wts, prompt=True, tm=tm_p, sub=256, tiles_per_seq=tps)

    tm_s = 256
    ntok_s = nb_s * seq_s
    xs = x_sample.reshape(ntok_s, D_MODEL)
    tabs_s = tuple(jnp.tile(t, (tm_s // seq_s, 1)) for t in _rope_tables(past_len + jnp.arange(seq_s), ATTN_SCALE))
    st = state_conv[layer]
    zrow = jnp.zeros((nb_s, seq_s - 1, CONV_DIM), F32)
    fix1 = jnp.concatenate([st[:, 1:2], zrow], axis=1).reshape(ntok_s, CONV_DIM)
    fix2 = jnp.concatenate([st, zrow[:, 1:]], axis=1).reshape(ntok_s, CONV_DIM)
    ckv_s, kr_s, cb_s, u_s, qabs_s, qr_s = _mixer_in(
        xs, ada_s[0], ada_s[1], in_wts, tabs_s, prompt=False, nb=nb_s, seq=seq_s, tm=tm_s, sub=tm_s,
        extra=(wukt, fix1, fix2))
    rows = N_HEADS * seq_s
    olat = _decode_attention(
        page_table, qabs_s.reshape(nb_s, rows, KV_LORA), qr_s.reshape(nb_s, rows, QK_ROPE),
        ckv_s.reshape(nb_s, seq_s, KV_LORA), kr_s.reshape(nb_s, seq_s, QK_ROPE),
        cache_kv_latent.reshape(n_pool, page, KV_LORA),
        jnp.swapaxes(cache_k_rope.reshape(n_pool, page, QK_ROPE), 1, 2))
    o_s = _uv_project(olat.reshape(nb_s, N_HEADS, seq_s, KV_LORA), wuv_heads)
    x1_s = _mixer_out(xs, ada_s[0], ada_s[1], ada_s[2], o_s, cb_s, out_wts, prompt=False, tm=tm_s, sub=tm_s, tiles_per_seq=1)
    y_s = _ffn(x1_s, ada_s[3], ada_s[4], ada_s[5], ffn_wts, prompt=False, tm=tm_s, sub=tm_s, tiles_per_seq=1)

    hist = CONV_WIDTH - 1
    return (y_p.reshape(nb_p, seq_p, D_MODEL),
            y_s.reshape(nb_s, seq_s, D_MODEL),
            ckv_p.reshape(1, nb_p, seq_p, KV_LORA),
            kr_p.reshape(1, nb_p, seq_p, QK_ROPE),
            u_p.reshape(1, nb_p, SUBLANES, CONV_DIM)[:, :, SUBLANES - hist:],
            ckv_s.reshape(1, nb_s, seq_s, KV_LORA),
            kr_s.reshape(1, nb_s, seq_s, QK_ROPE),
            u_s.reshape(1, nb_s, seq_s, CONV_DIM)[:, :, seq_s - hist:])
```

```python
import functools
import math

import jax
import jax.numpy as jnp
from jax import lax
from jax.experimental import pallas as pl
from jax.experimental.pallas import tpu as pltpu

D_MODEL = 1024
N_HEADS = 8
QK_NOPE = 64
QK_ROPE = 32
QK_HEAD = QK_NOPE + QK_ROPE
V_HEAD = 64
Q_LORA = 384
KV_LORA = 256
ROPE_THETA = 10000.0
ATTN_SCALE = QK_HEAD ** -0.5
CONV_DIM = 512
CONV_WIDTH = 3
D_FF = int(math.ceil(8 * D_MODEL / 3 / 256)) * 256
DEPTH = 1
ALPHA = (2.0 * DEPTH) ** 0.25
LN_EPS = 1e-5
RMS_EPS = 1e-6

LANES = 128
SUBLANES = 8
HEAD_SLAB = LANES
ROPE_LO = QK_NOPE
ROPE_HALF = QK_ROPE // 2
VMEM_LIMIT = 52 * 1024 * 1024

BF16 = jnp.bfloat16
F32 = jnp.float32
NT_DIMS = (((1,), (1,)), ((), ()))


def _params(n_axes=1, vmem=VMEM_LIMIT):
    return pltpu.CompilerParams(dimension_semantics=("arbitrary",) * n_axes, vmem_limit_bytes=vmem)


def _const_spec(shape):
    nd = len(shape)
    return pl.BlockSpec(shape, lambda *_: (0,) * nd, pipeline_mode=pl.Buffered(1))


def _mm(a, b):
    return jnp.dot(a, b, preferred_element_type=F32)


def _sigmoid(x):
    return 1.0 / (1.0 + jnp.exp(-x))


def _layer_norm(x, g, b):
    mu = jnp.mean(x, axis=-1, keepdims=True)
    xc = x - mu
    var = jnp.mean(xc * xc, axis=-1, keepdims=True)
    return xc * lax.rsqrt(var + LN_EPS) * g + b


def _rms_norm(x, g):
    return x * lax.rsqrt(jnp.mean(x * x, axis=-1, keepdims=True) + RMS_EPS) * g


def _rows(ref, r, tm):
    return ref[r, :] if ref.shape[0] == tm else ref[...]


def _pipelined(tm, sub, front, back):
    subs = [slice(r0, r0 + sub) for r0 in range(0, tm, sub)]
    pending = front(subs[0])
    for n, r in enumerate(subs):
        nxt = front(subs[n + 1]) if n + 1 < len(subs) else None
        back(r, *pending)
        pending = nxt


def _rope_slab(x, cos_t, sin_t):
    lane = lax.broadcasted_iota(jnp.int32, x.shape, 1)
    swapped = jnp.where(lane < ROPE_LO + ROPE_HALF,
                        pltpu.roll(x, LANES - ROPE_HALF, 1), pltpu.roll(x, ROPE_HALF, 1))
    return x * cos_t + swapped * sin_t


def _ada_kernel(c_ref, w_ref, b_ref, o_ref):
    c = c_ref[...]
    s = c * _sigmoid(c)
    o_ref[...] = _mm(s.astype(BF16), w_ref[...]) + b_ref[...]


def _ada_terms(c, w_ada, b_ada):
    n, tn = c.shape[0], 1536
    return pl.pallas_call(
        _ada_kernel,
        grid=(6 * D_MODEL // tn,),
        in_specs=[pl.BlockSpec((n, D_MODEL), lambda j: (0, 0)),
                  pl.BlockSpec((D_MODEL, tn), lambda j: (0, j)),
                  pl.BlockSpec((1, tn), lambda j: (0, j))],
        out_specs=pl.BlockSpec((n, tn), lambda j: (0, j)),
        out_shape=jax.ShapeDtypeStruct((n, 6 * D_MODEL), F32),
        compiler_params=_params(),
        name="ada_terms",
    )(c, w_ada, b_ada)


def _mixer_in_kernel(*refs, prompt, tm, sub, tiles_per_seq):
    if prompt:
        (x_ref, sh_ref, sc_ref, wa_ref, wb_ref, qg_ref, kvg_ref, wuq_ref, convw_ref,
         cq_ref, sq_ref, ck_ref, sk_ref, wuk_ref, wuvt_ref,
         ckv_ref, kr_ref, cb_ref, u_ref, q_ref, k_ref, vt_ref, ubuf) = refs
    else:
        (x_ref, sh_ref, sc_ref, wa_ref, wb_ref, qg_ref, kvg_ref, wuq_ref, convw_ref,
         cq_ref, sq_ref, ck_ref, sk_ref, wukt_ref, fix1_ref, fix2_ref,
         ckv_ref, kr_ref, cb_ref, u_ref, qabs_ref, qr_ref, ubuf) = refs
    i = pl.program_id(0)

    @pl.when(i % tiles_per_seq == 0)
    def _():
        ubuf[0:SUBLANES, :] = jnp.zeros((SUBLANES, CONV_DIM), F32)

    def front(r):
        h = (x_ref[r, :] * (1.0 + _rows(sc_ref, r, tm)) + _rows(sh_ref, r, tm)).astype(BF16)
        return _mm(h, wa_ref[...]), _mm(h, wb_ref[...])

    def back(r, za, zb):
        r0, n = r.start, r.stop - r.start
        q_lat = za[:, :Q_LORA]
        kv_lat = za[:, Q_LORA:Q_LORA + KV_LORA]
        kr_slab = za[:, Q_LORA + KV_LORA:]
        c_kv = _rms_norm(kv_lat, kvg_ref[...])
        ckv_ref[r, :] = c_kv
        kr_rot = _rope_slab(kr_slab, ck_ref[r, :], sk_ref[r, :])
        kr_ref[r, :] = kr_rot[:, ROPE_LO:ROPE_LO + QK_ROPE]

        q_all = _mm(_rms_norm(q_lat, qg_ref[...]).astype(BF16), wuq_ref[...])
        cq, sq = cq_ref[r, :], sq_ref[r, :]
        q_heads = [_rope_slab(q_all[:, hd * HEAD_SLAB:(hd + 1) * HEAD_SLAB], cq, sq) for hd in range(N_HEADS)]

        b_g = zb[:, :CONV_DIM]
        u = zb[:, CONV_DIM:2 * CONV_DIM] * zb[:, 2 * CONV_DIM:]
        ubuf[SUBLANES + r0:SUBLANES + r0 + n, :] = u
        u1 = ubuf[pl.ds(SUBLANES + r0 - 1, n), :]
        u2 = ubuf[pl.ds(SUBLANES + r0 - 2, n), :]
        if prompt:
            if r.stop == tm:
                ubuf[0:SUBLANES, :] = u[n - SUBLANES:, :]
                u_ref[...] = u[n - SUBLANES:, :]
        else:
            t = lax.broadcasted_iota(jnp.int32, (n, CONV_DIM), 0) % SUBLANES
            u1 = jnp.where(t == 0, fix1_ref[r, :], u1)
            u2 = jnp.where(t < 2, fix2_ref[r, :], u2)
            u_ref[r, :] = u
        cw = convw_ref[...]
        conv_y = u2 * cw[0:1, :] + u1 * cw[1:2, :] + u * cw[2:3, :]
        cb_ref[r, :] = (b_g * conv_y).astype(BF16)

        if prompt:
            c_bf = c_kv.astype(BF16)
            k_nope = _mm(c_bf, wuk_ref[...])
            vt_ref[:, r] = lax.dot_general(wuvt_ref[...], c_bf, NT_DIMS, preferred_element_type=F32).astype(BF16)
            for hd in range(N_HEADS):
                q_ref[hd, r, :] = q_heads[hd].astype(BF16)
                k_ref[hd, r, :] = (k_nope[:, hd * HEAD_SLAB:(hd + 1) * HEAD_SLAB] + kr_rot).astype(BF16)
        else:
            seqs = slice(r0 // SUBLANES, r.stop // SUBLANES)
            for hd in range(N_HEADS):
                qh = q_heads[hd]
                qabs = _mm(qh.astype(BF16), wukt_ref[hd])
                qabs_ref[seqs, hd] = qabs.reshape(n // SUBLANES, SUBLANES, KV_LORA)
                qr_ref[seqs, hd] = qh[:, ROPE_LO:ROPE_LO + QK_ROPE].reshape(n // SUBLANES, SUBLANES, QK_ROPE)

    _pipelined(tm, sub, front, back)


def _mixer_in(x2d, sh, sc, wts, tabs, *, prompt, nb, seq, tm, sub, extra):
    ntok = nb * seq
    tiles_per_seq = seq // tm if prompt else 1
    grid = (ntok // tm,)
    row = lambda w: pl.BlockSpec((tm, w), lambda i: (i, 0))
    if prompt:
        ada_spec = pl.BlockSpec((None, 1, D_MODEL), lambda i: (i // tiles_per_seq, 0, 0))
        tab_spec = pl.BlockSpec((tm, LANES), lambda i: (i % tiles_per_seq, 0))
    else:
        ada_spec = row(D_MODEL)
        tab_spec = pl.BlockSpec((tm, LANES), lambda i: (0, 0))
    wa, wb, qg, kvg, wuq, convw = wts
    in_specs = [row(D_MODEL), ada_spec, ada_spec,
                _const_spec(wa.shape), _const_spec(wb.shape), _const_spec(qg.shape), _const_spec(kvg.shape),
                _const_spec(wuq.shape), _const_spec(convw.shape),
                tab_spec, tab_spec, tab_spec, tab_spec]
    out_shape = [jax.ShapeDtypeStruct((ntok, KV_LORA), F32), jax.ShapeDtypeStruct((ntok, QK_ROPE), F32),
                 jax.ShapeDtypeStruct((ntok, CONV_DIM), BF16)]
    out_specs = [row(KV_LORA), row(QK_ROPE), row(CONV_DIM)]
    if prompt:
        wuk, wuv = extra
        in_specs += [_const_spec(wuk.shape), _const_spec(wuv.shape)]
        head_spec = pl.BlockSpec((None, N_HEADS, tm, HEAD_SLAB),
                                 lambda i: (i // tiles_per_seq, 0, i % tiles_per_seq, 0))
        out_shape += [jax.ShapeDtypeStruct((nb * SUBLANES, CONV_DIM), F32),
                      jax.ShapeDtypeStruct((nb, N_HEADS, seq, HEAD_SLAB), BF16),
                      jax.ShapeDtypeStruct((nb, N_HEADS, seq, HEAD_SLAB), BF16),
                      jax.ShapeDtypeStruct((nb, N_HEADS * V_HEAD, seq), BF16)]
        out_specs += [pl.BlockSpec((SUBLANES, CONV_DIM), lambda i: (i // tiles_per_seq, 0)),
                      head_spec, head_spec,
                      pl.BlockSpec((None, N_HEADS * V_HEAD, tm), lambda i: (i // tiles_per_seq, 0, i % tiles_per_seq))]
    else:
        wukt, fix1, fix2 = extra
        in_specs += [_const_spec(wukt.shape), row(CONV_DIM), row(CONV_DIM)]
        nseq = tm // SUBLANES
        out_shape += [jax.ShapeDtypeStruct((ntok, CONV_DIM), F32),
                      jax.ShapeDtypeStruct((nb, N_HEADS, SUBLANES, KV_LORA), F32),
                      jax.ShapeDtypeStruct((nb, N_HEADS, SUBLANES, QK_ROPE), F32)]
        out_specs += [row(CONV_DIM),
                      pl.BlockSpec((nseq, N_HEADS, SUBLANES, KV_LORA), lambda i: (i, 0, 0, 0)),
                      pl.BlockSpec((nseq, N_HEADS, SUBLANES, QK_ROPE), lambda i: (i, 0, 0, 0))]
    return pl.pallas_call(
        functools.partial(_mixer_in_kernel, prompt=prompt, tm=tm, sub=sub, tiles_per_seq=tiles_per_seq),
        grid=grid, in_specs=in_specs, out_specs=out_specs, out_shape=out_shape,
        scratch_shapes=[pltpu.VMEM((SUBLANES + tm, CONV_DIM), F32)],
        compiler_params=_params(),
        name="mixer_in_prompt" if prompt else "mixer_in_sample",
    )(x2d, sh, sc, wa, wb, qg, kvg, wuq, convw, *tabs, *extra)


def _prompt_attn_kernel(q_ref, k_ref, vt_ref, o_ref, *, seq, tq, heads):
    key_le_query = (lax.broadcasted_iota(jnp.int32, (tq, tq), 0) <= lax.broadcasted_iota(jnp.int32, (tq, tq), 1))

    def scores(i, hh):
        lo, hi = i * tq, (i + 1) * tq
        q = q_ref[hh, lo:hi, :]
        sd = lax.dot_general(k_ref[hh, lo:hi, :], q, NT_DIMS, preferred_element_type=F32)
        sd = jnp.where(key_le_query, sd, -jnp.inf)
        so = lax.dot_general(k_ref[hh, 0:lo, :], q, NT_DIMS, preferred_element_type=F32) if i > 0 else None
        return sd, so

    def softmax_pv(i, hh, sd, so):
        lo, hi = i * tq, (i + 1) * tq
        vt = vt_ref.at[hh * V_HEAD:(hh + 1) * V_HEAD, :]
        m = jnp.max(sd, axis=0, keepdims=True)
        if so is not None:
            m = jnp.maximum(m, jnp.max(so, axis=0, keepdims=True))
        pd = jnp.exp2(sd - m)
        l = jnp.sum(pd, axis=0, keepdims=True)
        acc = _mm(vt[:, lo:hi], pd.astype(BF16))
        if so is not None:
            po = jnp.exp2(so - m)
            l = l + jnp.sum(po, axis=0, keepdims=True)
            acc = acc + _mm(vt[:, 0:lo], po.astype(BF16))
        return acc * (1.0 / l)

    n_tiles = seq // tq
    units = [(i, hh) for i in reversed(range(n_tiles)) for hh in range(heads)]
    pending = scores(*units[0])
    outs = []
    for n, (i, hh) in enumerate(units):
        nxt = scores(*units[n + 1]) if n + 1 < len(units) else None
        outs.append(softmax_pv(i, hh, *pending))
        pending = nxt
        if hh % 2 == 1:
            o_ref[i * tq:(i + 1) * tq, (hh // 2) * LANES:(hh // 2 + 1) * LANES] = (
                jnp.concatenate(outs, axis=0).T.astype(BF16))
            outs = []


def _prompt_attention(q, k, vt, *, nb, seq, tq=512, heads=4):
    head_spec = pl.BlockSpec((None, heads, seq, HEAD_SLAB), lambda b, p: (b, p, 0, 0))
    return pl.pallas_call(
        functools.partial(_prompt_attn_kernel, seq=seq, tq=tq, heads=heads),
        grid=(nb, N_HEADS // heads),
        in_specs=[head_spec, head_spec, pl.BlockSpec((None, heads * V_HEAD, seq), lambda b, p: (b, p, 0))],
        out_specs=pl.BlockSpec((None, seq, heads * V_HEAD), lambda b, p: (b, 0, p)),
        out_shape=jax.ShapeDtypeStruct((nb, seq, N_HEADS * V_HEAD), BF16),
        compiler_params=_params(2),
        name="prompt_attention",
    )(q, k, vt)


def _decode_attn_kernel(pt_ref, qabs_ref, qr_ref, cnew_ref, krnew_ref, lat_hbm, ropet_hbm, olat_ref, *bufs,
                        n_pages, group, page, n_slots, n_steps, lanes):
    b = pl.program_id(0)
    n_groups = n_pages // group
    gk = group * page
    rows = N_HEADS * SUBLANES
    per_lane = 6 + 2 * n_groups
    lane_bufs = [bufs[j * per_lane:(j + 1) * per_lane] for j in range(lanes)]

    def page_copies(j, step, g):
        latbuf, ropebuf, sems = lane_bufs[j][:3]
        slot = g % n_slots
        out = []
        for p in range(group):
            pid = pt_ref[step * lanes + j, g * group + p]
            out.append(pltpu.make_async_copy(lat_hbm.at[pid], latbuf.at[slot, pl.ds(p * page, page)], sems.at[0, slot]))
            out.append(pltpu.make_async_copy(ropet_hbm.at[pid], ropebuf.at[slot, :, pl.ds(p * page, page)], sems.at[1, slot]))
        return out

    @pl.when(b == 0)
    def _():
        for g in range(n_slots):
            for j in range(lanes):
                for cp in page_copies(j, 0, g):
                    cp.start()

    def new_token_block(ref, j, width):
        return jnp.concatenate([ref[j], jnp.zeros((SUBLANES, width), F32)], axis=0).astype(BF16)

    def arrive(j, g):
        for cp in page_copies(j, b, g):
            cp.wait()

    def stage(j, g):
        latbuf, ropebuf = lane_bufs[j][:2]
        cbf, krbf = lane_bufs[j][5:5 + n_groups], lane_bufs[j][5 + n_groups:5 + 2 * n_groups]
        cbf[g][...] = latbuf[g % n_slots].astype(BF16)
        krbf[g][...] = ropebuf[g % n_slots].astype(BF16)

    def refill(j, g):
        nxt = g + n_slots
        if nxt < n_groups:
            for cp in page_copies(j, b, nxt):
                cp.start()
        else:
            @pl.when(b + 1 < n_steps)
            def _():
                for cp in page_copies(j, b + 1, nxt - n_groups):
                    cp.start()

    def phase1():
        qa = [qabs_ref[j].astype(BF16) for j in range(lanes)]
        qr = [qr_ref[j].astype(BF16) for j in range(lanes)]
        t_row = lax.broadcasted_iota(jnp.int32, (rows, 2 * SUBLANES), 0) % SUBLANES
        k_col = lax.broadcasted_iota(jnp.int32, (rows, 2 * SUBLANES), 1)
        for j in range(lanes):
            s_new = (lax.dot_general(qa[j], new_token_block(cnew_ref, j, KV_LORA), NT_DIMS, preferred_element_type=F32)
                     + lax.dot_general(qr[j], new_token_block(krnew_ref, j, QK_ROPE), NT_DIMS,
                                       preferred_element_type=F32))
            lane_bufs[j][4][...] = jnp.where(k_col <= t_row, s_new, -jnp.inf)
            arrive(j, 0)
        for j in range(lanes):
            stage(j, 0)
        m_lanes = [None] * lanes
        for g in range(n_groups):
            if g + 1 < n_groups:
                for j in range(lanes):
                    arrive(j, g + 1)
            for j in range(lanes):
                s_all = lane_bufs[j][3]
                cbf, krbf = lane_bufs[j][5:5 + n_groups], lane_bufs[j][5 + n_groups:5 + 2 * n_groups]
                s = lax.dot_general(qa[j], cbf[g][...], NT_DIMS, preferred_element_type=F32) + _mm(qr[j], krbf[g][...])
                s_all[:, g * gk:(g + 1) * gk] = s
                for k in range(gk // LANES):
                    blk = s[:, k * LANES:(k + 1) * LANES]
                    m_lanes[j] = blk if m_lanes[j] is None else jnp.maximum(m_lanes[j], blk)
                if g + 1 < n_groups:
                    stage(j, g + 1)
            for j in range(lanes):
                refill(j, g)
        for j in range(lanes):
            lane_bufs[j][5 + 2 * n_groups][...] = m_lanes[j]

    def phase2():
        m, l_new, l_lanes, acc = [], [], [None] * lanes, []
        for j in range(lanes):
            sn = lane_bufs[j][4][...]
            m_part = lane_bufs[j][5 + 2 * n_groups][...]
            m.append(jnp.maximum(jnp.max(sn, axis=-1, keepdims=True), jnp.max(m_part, axis=-1, keepdims=True)))
            p = jnp.exp(sn - m[j])
            l_new.append(jnp.sum(p, axis=-1, keepdims=True))
            acc.append(_mm(p.astype(BF16), new_token_block(cnew_ref, j, KV_LORA)))
        units = [(g, j) for g in range(n_groups) for j in range(lanes)]
        pending = None
        for n in range(len(units) + 1):
            if n < len(units):
                g, j = units[n]
                p = jnp.exp(lane_bufs[j][3][:, g * gk:(g + 1) * gk] - m[j])
                for k in range(gk // LANES):
                    blk = p[:, k * LANES:(k + 1) * LANES]
                    l_lanes[j] = blk if l_lanes[j] is None else l_lanes[j] + blk
                nxt = (g, j, p.astype(BF16))
            else:
                nxt = None
            if pending is not None:
                g0, j0, p0 = pending
                acc[j0] = acc[j0] + _mm(p0, lane_bufs[j0][5 + g0][...])
            pending = nxt
        for j in range(lanes):
            l = l_new[j] + jnp.sum(l_lanes[j], axis=-1, keepdims=True)
            olat_ref[j] = acc[j] * (1.0 / l)

    pl.when(b >= 0)(phase1)
    pl.when(b < n_steps)(phase2)


def _decode_attention(page_table, qabs, qr, c_new, kr_new, cache_lat, cache_rope_t, *, group=16, n_slots=2, lanes=2):
    nb, n_pages = page_table.shape
    page = cache_lat.shape[1]
    rows = N_HEADS * SUBLANES
    n_groups = n_pages // group
    assert n_pages % group == 0 and n_groups % n_slots == 0 and nb % lanes == 0
    per_step = lambda w, r: pl.BlockSpec((lanes, r, w), lambda b, pt: (b, 0, 0))
    lane_scratch = ([pltpu.VMEM((n_slots, group * page, KV_LORA), F32),
                     pltpu.VMEM((n_slots, QK_ROPE, group * page), F32),
                     pltpu.SemaphoreType.DMA((2, n_slots)),
                     pltpu.VMEM((rows, n_pages * page), F32),
                     pltpu.VMEM((rows, 2 * SUBLANES), F32)]
                    + [pltpu.VMEM((group * page, KV_LORA), BF16)] * n_groups
                    + [pltpu.VMEM((QK_ROPE, group * page), BF16)] * n_groups
                    + [pltpu.VMEM((rows, LANES), F32)])
    grid_spec = pltpu.PrefetchScalarGridSpec(
        num_scalar_prefetch=1,
        grid=(nb // lanes,),
        in_specs=[per_step(KV_LORA, rows), per_step(QK_ROPE, rows), per_step(KV_LORA, SUBLANES),
                  per_step(QK_ROPE, SUBLANES),
                  pl.BlockSpec(memory_space=pl.ANY), pl.BlockSpec(memory_space=pl.ANY)],
        out_specs=per_step(KV_LORA, rows),
        scratch_shapes=lane_scratch * lanes,
    )
    return pl.pallas_call(
        functools.partial(_decode_attn_kernel, n_pages=n_pages, group=group, page=page, n_slots=n_slots,
                          n_steps=nb // lanes, lanes=lanes),
        grid_spec=grid_spec,
        out_shape=jax.ShapeDtypeStruct((nb, rows, KV_LORA), F32),
        compiler_params=_params(),
        name="decode_attention",
    )(page_table, qabs, qr, c_new, kr_new, cache_lat, cache_rope_t)


def _uv_kernel(olat_ref, wuv_ref, o_ref):
    n = olat_ref.shape[0] * SUBLANES
    for hd in range(N_HEADS):
        x = olat_ref[:, hd].reshape(n, KV_LORA).astype(BF16)
        o_ref[:, hd * V_HEAD:(hd + 1) * V_HEAD] = _mm(x, wuv_ref[hd]).astype(BF16)


def _uv_project(olat, wuv_heads):
    nb = olat.shape[0]
    return pl.pallas_call(
        _uv_kernel,
        grid=(1,),
        in_specs=[pl.BlockSpec(olat.shape, lambda i: (0, 0, 0, 0)),
                  pl.BlockSpec(wuv_heads.shape, lambda i: (0, 0, 0))],
        out_specs=pl.BlockSpec((nb * SUBLANES, N_HEADS * V_HEAD), lambda i: (0, 0)),
        out_shape=jax.ShapeDtypeStruct((nb * SUBLANES, N_HEADS * V_HEAD), BF16),
        compiler_params=_params(),
        name="uv_project",
    )(olat, wuv_heads)


def _mixer_out_kernel(x_ref, sh_ref, sc_ref, g1_ref, o_ref, cb_ref, wg_ref, woa_ref, woc_ref, wo_ref,
                      lng_ref, lnb_ref, x1_ref, *, tm, sub):
    def front(r):
        h = (x_ref[r, :] * (1.0 + _rows(sc_ref, r, tm)) + _rows(sh_ref, r, tm)).astype(BF16)
        return _mm(h, wg_ref[...]), _mm(o_ref[r, :], woa_ref[...]), _mm(cb_ref[r, :], woc_ref[...])

    def back(r, gates, y_a, y_c):
        mixed = _sigmoid(gates[:, :D_MODEL]) * y_a + _sigmoid(gates[:, D_MODEL:]) * y_c
        mix = _mm(mixed.astype(BF16), wo_ref[...])
        x1_ref[r, :] = _layer_norm(ALPHA * x_ref[r, :] + _rows(g1_ref, r, tm) * mix, lng_ref[...], lnb_ref[...])

    _pipelined(tm, sub, front, back)


def _mixer_out(x2d, sh, sc, g1, o, cb, wts, *, prompt, tm, sub, tiles_per_seq):
    ntok = x2d.shape[0]
    row = lambda w: pl.BlockSpec((tm, w), lambda i: (i, 0))
    ada_spec = (pl.BlockSpec((None, 1, D_MODEL), lambda i: (i // tiles_per_seq, 0, 0)) if prompt else row(D_MODEL))
    return pl.pallas_call(
        functools.partial(_mixer_out_kernel, tm=tm, sub=sub),
        grid=(ntok // tm,),
        in_specs=[row(D_MODEL), ada_spec, ada_spec, ada_spec, row(N_HEADS * V_HEAD), row(CONV_DIM)]
                 + [_const_spec(w.shape) for w in wts],
        out_specs=row(D_MODEL),
        out_shape=jax.ShapeDtypeStruct((ntok, D_MODEL), F32),
        compiler_params=_params(),
        name="mixer_out_prompt" if prompt else "mixer_out_sample",
    )(x2d, sh, sc, g1, o, cb, *wts)


def _ffn_kernel(x1_ref, sh_ref, sc_ref, g2_ref, w1_ref, w3_ref, w2_ref, lng_ref, lnb_ref, y_ref, *, tm, sub):
    def front(r):
        h2 = (x1_ref[r, :] * (1.0 + _rows(sc_ref, r, tm)) + _rows(sh_ref, r, tm)).astype(BF16)
        return _mm(h2, w1_ref[...]), _mm(h2, w3_ref[...])

    def back(r, a, g):
        gated = (a * _sigmoid(a)) * g
        f = _mm(gated.astype(BF16), w2_ref[...])
        y_ref[r, :] = _layer_norm(ALPHA * x1_ref[r, :] + _rows(g2_ref, r, tm) * f, lng_ref[...], lnb_ref[...])

    _pipelined(tm, sub, front, back)


def _ffn(x1, sh, sc, g2, wts, *, prompt, tm, sub, tiles_per_seq):
    ntok = x1.shape[0]
    row = lambda w: pl.BlockSpec((tm, w), lambda i: (i, 0))
    ada_spec = (pl.BlockSpec((None, 1, D_MODEL), lambda i: (i // tiles_per_seq, 0, 0)) if prompt else row(D_MODEL))
    return pl.pallas_call(
        functools.partial(_ffn_kernel, tm=tm, sub=sub),
        grid=(ntok // tm,),
        in_specs=[row(D_MODEL), ada_spec, ada_spec, ada_spec] + [_const_spec(w.shape) for w in wts],
        out_specs=row(D_MODEL),
        out_shape=jax.ShapeDtypeStruct((ntok, D_MODEL), F32),
        compiler_params=_params(),
        name="ffn_prompt" if prompt else "ffn_sample",
    )(x1, sh, sc, g2, *wts)


def _rope_tables(pos, scale_q):
    inv = ROPE_THETA ** (-jnp.arange(0, QK_ROPE, 2, dtype=F32) / QK_ROPE)
    ang = pos.astype(F32)[:, None] * inv[None, :]
    cos, sin = jnp.cos(ang), jnp.sin(ang)
    n = pos.shape[0]
    z_lo = jnp.zeros((n, ROPE_LO), F32)
    z_hi = jnp.zeros((n, LANES - ROPE_LO - QK_ROPE), F32)
    cos_k = jnp.concatenate([z_lo, cos, cos, z_hi], axis=1)
    sin_k = jnp.concatenate([z_lo, -sin, sin, z_hi], axis=1)
    cos_q = jnp.concatenate([jnp.ones((n, ROPE_LO), F32), cos, cos, z_hi], axis=1) * scale_q
    sin_q = sin_k * scale_q
    return cos_q, sin_q, cos_k, sin_k


def _head_groups(w, width):
    k = w.shape[0]
    return jnp.pad(w, ((0, 0), (0, 0), (0, HEAD_SLAB - width))).reshape(k, N_HEADS * HEAD_SLAB)


def kernel(x_prompt, x_sample, cache_kv_latent, cache_k_rope, state_conv, page_table, c_prompt, c_sample,
           w_ada, b_ada, w_in, q_norm_g, kv_norm_g, w_uq, w_ukv, w_oa, conv_w, w_oc, w_o,
           ln1_g, ln1_b, w_ff1, w_ff3, w_ff2, ln2_g, ln2_b):
    assert w_ada.shape[0] == DEPTH == 1
    nb_p, seq_p, _ = x_prompt.shape
    nb_s, seq_s, _ = x_sample.shape
    assert seq_s == SUBLANES
    n_pool, page = cache_kv_latent.shape[1], cache_kv_latent.shape[2]
    past_len = page_table.shape[1] * page
    layer = 0

    s0, s1, s2, s3 = Q_LORA, Q_LORA + KV_LORA, Q_LORA + KV_LORA + QK_ROPE, Q_LORA + KV_LORA + QK_ROPE + 3 * CONV_DIM
    w_in_l = w_in[layer]
    w_kr_group = jnp.pad(w_in_l[:, s1:s2], ((0, 0), (ROPE_LO, LANES - ROPE_LO - QK_ROPE)))
    wa = jnp.concatenate([w_in_l[:, :s1], w_kr_group], axis=1).astype(BF16)
    wb = w_in_l[:, s2:s3].astype(BF16)
    wg = w_in_l[:, s3:].astype(BF16)
    wuq = _head_groups(w_uq[layer].reshape(Q_LORA, N_HEADS, QK_HEAD), QK_HEAD).astype(BF16)
    w_uk = w_ukv[layer][:, :, :QK_NOPE]
    w_uv = w_ukv[layer][:, :, QK_NOPE:]
    wuk_groups = _head_groups(w_uk, QK_NOPE).astype(BF16)
    wuv_t = w_uv.reshape(KV_LORA, N_HEADS * V_HEAD).T.astype(BF16)
    wuv_heads = w_uv.transpose(1, 0, 2).astype(BF16)
    wukt = jnp.pad(w_uk.transpose(1, 2, 0), ((0, 0), (0, HEAD_SLAB - QK_NOPE), (0, 0))).astype(BF16)
    qg = q_norm_g[layer][None, :]
    kvg = kv_norm_g[layer][None, :]
    convw = conv_w[layer]
    in_wts = (wa, wb, qg, kvg, wuq, convw)
    out_wts = (wg, w_oa[layer].astype(BF16), w_oc[layer].astype(BF16), w_o[layer].astype(BF16),
               ln1_g[layer][None, :], ln1_b[layer][None, :])
    ffn_wts = (w_ff1[layer].astype(BF16), w_ff3[layer].astype(BF16), w_ff2[layer].astype(BF16),
               ln2_g[layer][None, :], ln2_b[layer][None, :])

    ada = _ada_terms(jnp.concatenate([c_prompt, c_sample], axis=0), w_ada[layer].astype(BF16), b_ada[layer][None, :])
    ada_p = [a[:, None, :] for a in jnp.split(ada[:nb_p], 6, axis=-1)]
    ada_s = [jnp.repeat(a, seq_s, axis=0) for a in jnp.split(ada[nb_p:], 6, axis=-1)]

    tm_p = 1024
    tps = seq_p // tm_p
    xp = x_prompt.reshape(nb_p * seq_p, D_MODEL)
    tabs_p = _rope_tables(jnp.arange(seq_p), ATTN_SCALE * math.log2(math.e))
    ckv_p, kr_p, cb_p, u_p, q_p, k_p, vt_p = _mixer_in(
        xp, ada_p[0], ada_p[1], in_wts, tabs_p, prompt=True, nb=nb_p, seq=seq_p, tm=1024, sub=256,
        extra=(wuk_groups, wuv_t))
    o_p = _prompt_attention(q_p, k_p, vt_p, nb=nb_p, seq=seq_p)
    x1_p = _mixer_out(xp, ada_p[0], ada_p[1], ada_p[2], o_p.reshape(nb_p * seq_p, N_HEADS * V_HEAD), cb_p, out_wts,
                      prompt=True, tm=tm_p, sub=256, tiles_per_seq=tps)
    y_p = _ffn(x1_p, ada_p[3], ada_p[4], ada_p[5], ffn_wts, prompt=True, tm=tm_p, sub=256, tiles_per_seq=tps)

    tm_s = 256
    ntok_s = nb_s * seq_s
    xs = x_sample.reshape(ntok_s, D_MODEL)
    tabs_s = tuple(jnp.tile(t, (tm_s // seq_s, 1)) for t in _rope_tables(past_len + jnp.arange(seq_s), ATTN_SCALE))
    st = state_conv[layer]
    zrow = jnp.zeros((nb_s, seq_s - 1, CONV_DIM), F32)
    fix1 = jnp.concatenate([st[:, 1:2], zrow], axis=1).reshape(ntok_s, CONV_DIM)
    fix2 = jnp.concatenate([st, zrow[:, 1:]], axis=1).reshape(ntok_s, CONV_DIM)
    ckv_s, kr_s, cb_s, u_s, qabs_s, qr_s = _mixer_in(
        xs, ada_s[0], ada_s[1], in_wts, tabs_s, prompt=False, nb=nb_s, seq=seq_s, tm=tm_s, sub=tm_s,
        extra=(wukt, fix1, fix2))
    rows = N_HEADS * seq_s
    olat = _decode_attention(
        page_table, qabs_s.reshape(nb_s, rows, KV_LORA), qr_s.reshape(nb_s, rows, QK_ROPE),
        ckv_s.reshape(nb_s, seq_s, KV_LORA), kr_s.reshape(nb_s, seq_s, QK_ROPE),
        cache_kv_latent.reshape(n_pool, page, KV_LORA),
        jnp.swapaxes(cache_k_rope.reshape(n_pool, page, QK_ROPE), 1, 2))
    o_s = _uv_project(olat.reshape(nb_s, N_HEADS, seq_s, KV_LORA), wuv_heads)
    x1_s = _mixer_out(xs, ada_s[0], ada_s[1], ada_s[2], o_s, cb_s, out_wts, prompt=False, tm=tm_s, sub=tm_s, tiles_per_seq=1)
    y_s = _ffn(x1_s, ada_s[3], ada_s[4], ada_s[5], ffn_wts, prompt=False, tm=tm_s, sub=tm_s, tiles_per_seq=1)

    hist = CONV_WIDTH - 1
    return (y_p.reshape(nb_p, seq_p, D_MODEL),
            y_s.reshape(nb_s, seq_s, D_MODEL),
            ckv_p.reshape(1, nb_p, seq_p, KV_LORA),
            kr_p.reshape(1, nb_p, seq_p, QK_ROPE),
            u_p.reshape(1, nb_p, SUBLANES, CONV_DIM)[:, :, SUBLANES - hist:],
            ckv_s.reshape(1, nb_s, seq_s, KV_LORA),
            kr_s.reshape(1, nb_s, seq_s, QK_ROPE),
            u_s.reshape(1, nb_s, seq_s, CONV_DIM)[:, :, seq_s - hist:])
```

```python
import functools
import math

import jax
import jax.numpy as jnp
from jax import lax
from jax.experimental import pallas as pl
from jax.experimental.pallas import tpu as pltpu

D_MODEL = 1024
N_HEADS = 8
QK_NOPE = 64
QK_ROPE = 32
QK_HEAD = QK_NOPE + QK_ROPE
V_HEAD = 64
Q_LORA = 384
KV_LORA = 256
ROPE_THETA = 10000.0
ATTN_SCALE = QK_HEAD ** -0.5
CONV_DIM = 512
CONV_WIDTH = 3
D_FF = int(math.ceil(8 * D_MODEL / 3 / 256)) * 256
DEPTH = 1
ALPHA = (2.0 * DEPTH) ** 0.25
LN_EPS = 1e-5
RMS_EPS = 1e-6

LANES = 128
SUBLANES = 8
HEAD_SLAB = LANES
ROPE_LO = QK_NOPE
ROPE_HALF = QK_ROPE // 2
VMEM_LIMIT = 52 * 1024 * 1024

BF16 = jnp.bfloat16
F32 = jnp.float32
NT_DIMS = (((1,), (1,)), ((), ()))


def _params(n_axes=1, vmem=VMEM_LIMIT):
    return pltpu.CompilerParams(dimension_semantics=("arbitrary",) * n_axes, vmem_limit_bytes=vmem)


def _const_spec(shape):
    nd = len(shape)
    return pl.BlockSpec(shape, lambda *_: (0,) * nd, pipeline_mode=pl.Buffered(1))


def _mm(a, b):
    return jnp.dot(a, b, preferred_element_type=F32)


def _sigmoid(x):
    return 1.0 / (1.0 + jnp.exp(-x))


def _layer_norm(x, g, b):
    mu = jnp.mean(x, axis=-1, keepdims=True)
    xc = x - mu
    var = jnp.mean(xc * xc, axis=-1, keepdims=True)
    return xc * lax.rsqrt(var + LN_EPS) * g + b


def _rms_norm(x, g):
    return x * lax.rsqrt(jnp.mean(x * x, axis=-1, keepdims=True) + RMS_EPS) * g


def _rows(ref, r, tm):
    return ref[r, :] if ref.shape[0] == tm else ref[...]


def _pipelined(tm, sub, front, back):
    subs = [slice(r0, r0 + sub) for r0 in range(0, tm, sub)]
    pending = front(subs[0])
    for n, r in enumerate(subs):
        nxt = front(subs[n + 1]) if n + 1 < len(subs) else None
        back(r, *pending)
        pending = nxt


def _rope_slab(x, cos_t, sin_t):
    lane = lax.broadcasted_iota(jnp.int32, x.shape, 1)
    swapped = jnp.where(lane < ROPE_LO + ROPE_HALF,
                        pltpu.roll(x, LANES - ROPE_HALF, 1), pltpu.roll(x, ROPE_HALF, 1))
    return x * cos_t + swapped * sin_t


def _ada_kernel(c_ref, w_ref, b_ref, o_ref):
    c = c_ref[...]
    s = c * _sigmoid(c)
    o_ref[...] = _mm(s.astype(BF16), w_ref[...]) + b_ref[...]


def _ada_terms(c, w_ada, b_ada):
    n, tn = c.shape[0], 1536
    return pl.pallas_call(
        _ada_kernel,
        grid=(6 * D_MODEL // tn,),
        in_specs=[pl.BlockSpec((n, D_MODEL), lambda j: (0, 0)),
                  pl.BlockSpec((D_MODEL, tn), lambda j: (0, j)),
                  pl.BlockSpec((1, tn), lambda j: (0, j))],
        out_specs=pl.BlockSpec((n, tn), lambda j: (0, j)),
        out_shape=jax.ShapeDtypeStruct((n, 6 * D_MODEL), F32),
        compiler_params=_params(),
        name="ada_terms",
    )(c, w_ada, b_ada)


def _mixer_in_kernel(*refs, prompt, tm, sub, tiles_per_seq):
    if prompt:
        (x_ref, sh_ref, sc_ref, wa_ref, wb_ref, qg_ref, kvg_ref, wuq_ref, convw_ref,
         cq_ref, sq_ref, ck_ref, sk_ref, wuk_ref, wuvt_ref,
         ckv_ref, kr_ref, cb_ref, u_ref, q_ref, k_ref, vt_ref, ubuf) = refs
    else:
        (x_ref, sh_ref, sc_ref, wa_ref, wb_ref, qg_ref, kvg_ref, wuq_ref, convw_ref,
         cq_ref, sq_ref, ck_ref, sk_ref, wukt_ref, fix1_ref, fix2_ref,
         ckv_ref, kr_ref, cb_ref, u_ref, qabs_ref, qr_ref, ubuf) = refs
    i = pl.program_id(0)

    @pl.when(i % tiles_per_seq == 0)
    def _():
        ubuf[0:SUBLANES, :] = jnp.zeros((SUBLANES, CONV_DIM), F32)

    def front(r):
        h = (x_ref[r, :] * (1.0 + _rows(sc_ref, r, tm)) + _rows(sh_ref, r, tm)).astype(BF16)
        return _mm(h, wa_ref[...]), _mm(h, wb_ref[...])

    def back(r, za, zb):
        r0, n = r.start, r.stop - r.start
        q_lat = za[:, :Q_LORA]
        kv_lat = za[:, Q_LORA:Q_LORA + KV_LORA]
        kr_slab = za[:, Q_LORA + KV_LORA:]
        c_kv = _rms_norm(kv_lat, kvg_ref[...])
        ckv_ref[r, :] = c_kv
        kr_rot = _rope_slab(kr_slab, ck_ref[r, :], sk_ref[r, :])
        kr_ref[r, :] = kr_rot[:, ROPE_LO:ROPE_LO + QK_ROPE]

        q_all = _mm(_rms_norm(q_lat, qg_ref[...]).astype(BF16), wuq_ref[...])
        cq, sq = cq_ref[r, :], sq_ref[r, :]
        q_heads = [_rope_slab(q_all[:, hd * HEAD_SLAB:(hd + 1) * HEAD_SLAB], cq, sq) for hd in range(N_HEADS)]

        b_g = zb[:, :CONV_DIM]
        u = zb[:, CONV_DIM:2 * CONV_DIM] * zb[:, 2 * CONV_DIM:]
        ubuf[SUBLANES + r0:SUBLANES + r0 + n, :] = u
        u1 = ubuf[pl.ds(SUBLANES + r0 - 1, n), :]
        u2 = ubuf[pl.ds(SUBLANES + r0 - 2, n), :]
        if prompt:
            if r.stop == tm:
                ubuf[0:SUBLANES, :] = u[n - SUBLANES:, :]
                u_ref[...] = u[n - SUBLANES:, :]
        else:
            t = lax.broadcasted_iota(jnp.int32, (n, CONV_DIM), 0) % SUBLANES
            u1 = jnp.where(t == 0, fix1_ref[r, :], u1)
            u2 = jnp.where(t < 2, fix2_ref[r, :], u2)
            u_ref[r, :] = u
        cw = convw_ref[...]
        conv_y = u2 * cw[0:1, :] + u1 * cw[1:2, :] + u * cw[2:3, :]
        cb_ref[r, :] = (b_g * conv_y).astype(BF16)

        if prompt:
            c_bf = c_kv.astype(BF16)
            k_nope = _mm(c_bf, wuk_ref[...])
            vt_ref[:, r] = lax.dot_general(wuvt_ref[...], c_bf, NT_DIMS, preferred_element_type=F32).astype(BF16)
            for hd in range(N_HEADS):
                q_ref[hd, r, :] = q_heads[hd].astype(BF16)
                k_ref[hd, r, :] = (k_nope[:, hd * HEAD_SLAB:(hd + 1) * HEAD_SLAB] + kr_rot).astype(BF16)
        else:
            seqs = slice(r0 // SUBLANES, r.stop // SUBLANES)
            for hd in range(N_HEADS):
                qh = q_heads[hd]
                qabs = _mm(qh.astype(BF16), wukt_ref[hd])
                qabs_ref[seqs, hd] = qabs.reshape(n // SUBLANES, SUBLANES, KV_LORA)
                qr_ref[seqs, hd] = qh[:, ROPE_LO:ROPE_LO + QK_ROPE].reshape(n // SUBLANES, SUBLANES, QK_ROPE)

    _pipelined(tm, sub, front, back)


def _mixer_in(x2d, sh, sc, wts, tabs, *, prompt, nb, seq, tm, sub, extra):
    ntok = nb * seq
    tiles_per_seq = seq // tm if prompt else 1
    grid = (ntok // tm,)
    row = lambda w: pl.BlockSpec((tm, w), lambda i: (i, 0))
    if prompt:
        ada_spec = pl.BlockSpec((None, 1, D_MODEL), lambda i: (i // tiles_per_seq, 0, 0))
        tab_spec = pl.BlockSpec((tm, LANES), lambda i: (i % tiles_per_seq, 0))
    else:
        ada_spec = row(D_MODEL)
        tab_spec = pl.BlockSpec((tm, LANES), lambda i: (0, 0))
    wa, wb, qg, kvg, wuq, convw = wts
    in_specs = [row(D_MODEL), ada_spec, ada_spec,
                _const_spec(wa.shape), _const_spec(wb.shape), _const_spec(qg.shape), _const_spec(kvg.shape),
                _const_spec(wuq.shape), _const_spec(convw.shape),
                tab_spec, tab_spec, tab_spec, tab_spec]
    out_shape = [jax.ShapeDtypeStruct((ntok, KV_LORA), F32), jax.ShapeDtypeStruct((ntok, QK_ROPE), F32),
                 jax.ShapeDtypeStruct((ntok, CONV_DIM), BF16)]
    out_specs = [row(KV_LORA), row(QK_ROPE), row(CONV_DIM)]
    if prompt:
        wuk, wuv = extra
        in_specs += [_const_spec(wuk.shape), _const_spec(wuv.shape)]
        head_spec = pl.BlockSpec((None, N_HEADS, tm, HEAD_SLAB),
                                 lambda i: (i // tiles_per_seq, 0, i % tiles_per_seq, 0))
        out_shape += [jax.ShapeDtypeStruct((nb * SUBLANES, CONV_DIM), F32),
                      jax.ShapeDtypeStruct((nb, N_HEADS, seq, HEAD_SLAB), BF16),
                      jax.ShapeDtypeStruct((nb, N_HEADS, seq, HEAD_SLAB), BF16),
                      jax.ShapeDtypeStruct((nb, N_HEADS * V_HEAD, seq), BF16)]
        out_specs += [pl.BlockSpec((SUBLANES, CONV_DIM), lambda i: (i // tiles_per_seq, 0)),
                      head_spec, head_spec,
                      pl.BlockSpec((None, N_HEADS * V_HEAD, tm), lambda i: (i // tiles_per_seq, 0, i % tiles_per_seq))]
    else:
        wukt, fix1, fix2 = extra
        in_specs += [_const_spec(wukt.shape), row(CONV_DIM), row(CONV_DIM)]
        nseq = tm // SUBLANES
        out_shape += [jax.ShapeDtypeStruct((ntok, CONV_DIM), F32),
                      jax.ShapeDtypeStruct((nb, N_HEADS, SUBLANES, KV_LORA), F32),
                      jax.ShapeDtypeStruct((nb, N_HEADS, SUBLANES, QK_ROPE), F32)]
        out_specs += [row(CONV_DIM),
                      pl.BlockSpec((nseq, N_HEADS, SUBLANES, KV_LORA), lambda i: (i, 0, 0, 0)),
                      pl.BlockSpec((nseq, N_HEADS, SUBLANES, QK_ROPE), lambda i: (i, 0, 0, 0))]
    return pl.pallas_call(
        functools.partial(_mixer_in_kernel, prompt=prompt, tm=tm, sub=sub, tiles_per_seq=tiles_per_seq),
        grid=grid, in_specs=in_specs, out_specs=out_specs, out_shape=out_shape,
        scratch_shapes=[pltpu.VMEM((SUBLANES + tm, CONV_DIM), F32)],
        compiler_params=_params(),
        name="mixer_in_prompt" if prompt else "mixer_in_sample",
    )(x2d, sh, sc, wa, wb, qg, kvg, wuq, convw, *tabs, *extra)


def _prompt_attn_kernel(q_ref, k_ref, vt_ref, o_ref, *, seq, tq, heads):
    key_le_query = (lax.broadcasted_iota(jnp.int32, (tq, tq), 0) <= lax.broadcasted_iota(jnp.int32, (tq, tq), 1))

    def scores(i, hh):
        lo, hi = i * tq, (i + 1) * tq
        q = q_ref[hh, lo:hi, :]
        sd = lax.dot_general(k_ref[hh, lo:hi, :], q, NT_DIMS, preferred_element_type=F32)
        sd = jnp.where(key_le_query, sd, -jnp.inf)
        so = lax.dot_general(k_ref[hh, 0:lo, :], q, NT_DIMS, preferred_element_type=F32) if i > 0 else None
        return sd, so

    def softmax_pv(i, hh, sd, so):
        lo, hi = i * tq, (i + 1) * tq
        vt = vt_ref.at[hh * V_HEAD:(hh + 1) * V_HEAD, :]
        m = jnp.max(sd, axis=0, keepdims=True)
        if so is not None:
            m = jnp.maximum(m, jnp.max(so, axis=0, keepdims=True))
        pd = jnp.exp2(sd - m)
        l = jnp.sum(pd, axis=0, keepdims=True)
        acc = _mm(vt[:, lo:hi], pd.astype(BF16))
        if so is not None:
            po = jnp.exp2(so - m)
            l = l + jnp.sum(po, axis=0, keepdims=True)
            acc = acc + _mm(vt[:, 0:lo], po.astype(BF16))
        return acc * (1.0 / l)

    n_tiles = seq // tq
    units = [(i, hh) for i in reversed(range(n_tiles)) for hh in range(heads)]
    pending = scores(*units[0])
    outs = []
    for n, (i, hh) in enumerate(units):
        nxt = scores(*units[n + 1]) if n + 1 < len(units) else None
        outs.append(softmax_pv(i, hh, *pending))
        pending = nxt
        if hh % 2 == 1:
            o_ref[i * tq:(i + 1) * tq, (hh // 2) * LANES:(hh // 2 + 1) * LANES] = (
                jnp.concatenate(outs, axis=0).T.astype(BF16))
            outs = []


def _prompt_attention(q, k, vt, *, nb, seq, tq=512, heads=4):
    head_spec = pl.BlockSpec((None, heads, seq, HEAD_SLAB), lambda b, p: (b, p, 0, 0))
    return pl.pallas_call(
        functools.partial(_prompt_attn_kernel, seq=seq, tq=tq, heads=heads),
        grid=(nb, N_HEADS // heads),
        in_specs=[head_spec, head_spec, pl.BlockSpec((None, heads * V_HEAD, seq), lambda b, p: (b, p, 0))],
        out_specs=pl.BlockSpec((None, seq, heads * V_HEAD), lambda b, p: (b, 0, p)),
        out_shape=jax.ShapeDtypeStruct((nb, seq, N_HEADS * V_HEAD), BF16),
        compiler_params=_params(2),
        name="prompt_attention",
    )(q, k, vt)


def _decode_attn_kernel(pt_ref, qabs_ref, qr_ref, cnew_ref, krnew_ref, lat_hbm, ropet_hbm, olat_ref, *bufs,
                        n_pages, group, page, n_slots, n_steps, lanes):
    b = pl.program_id(0)
    n_groups = n_pages // group
    gk = group * page
    rows = N_HEADS * SUBLANES
    per_lane = 6 + 2 * n_groups
    lane_bufs = [bufs[j * per_lane:(j + 1) * per_lane] for j in range(lanes)]

    def page_copies(j, step, g):
        latbuf, ropebuf, sems = lane_bufs[j][:3]
        slot = g % n_slots
        out = []
        for p in range(group):
            pid = pt_ref[step * lanes + j, g * group + p]
            out.append(pltpu.make_async_copy(lat_hbm.at[pid], latbuf.at[slot, pl.ds(p * page, page)], sems.at[0, slot]))
            out.append(pltpu.make_async_copy(ropet_hbm.at[pid], ropebuf.at[slot, :, pl.ds(p * page, page)], sems.at[1, slot]))
        return out

    @pl.when(b == 0)
    def _():
        for g in range(n_slots):
            for j in range(lanes):
                for cp in page_copies(j, 0, g):
                    cp.start()

    def new_token_block(ref, j, width):
        return jnp.concatenate([ref[j], jnp.zeros((SUBLANES, width), F32)], axis=0).astype(BF16)

    def arrive(j, g):
        for cp in page_copies(j, b, g):
            cp.wait()

    def stage(j, g):
        latbuf, ropebuf = lane_bufs[j][:2]
        cbf, krbf = lane_bufs[j][5:5 + n_groups], lane_bufs[j][5 + n_groups:5 + 2 * n_groups]
        cbf[g][...] = latbuf[g % n_slots].astype(BF16)
        krbf[g][...] = ropebuf[g % n_slots].astype(BF16)

    def refill(j, g):
        nxt = g + n_slots
        if nxt < n_groups:
            for cp in page_copies(j, b, nxt):
                cp.start()
        else:
            @pl.when(b + 1 < n_steps)
            def _():
                for cp in page_copies(j, b + 1, nxt - n_groups):
                    cp.start()

    def phase1():
        qa = [qabs_ref[j].astype(BF16) for j in range(lanes)]
        qr = [qr_ref[j].astype(BF16) for j in range(lanes)]
        t_row = lax.broadcasted_iota(jnp.int32, (rows, 2 * SUBLANES), 0) % SUBLANES
        k_col = lax.broadcasted_iota(jnp.int32, (rows, 2 * SUBLANES), 1)
        for j in range(lanes):
            s_new = (lax.dot_general(qa[j], new_token_block(cnew_ref, j, KV_LORA), NT_DIMS, preferred_element_type=F32)
                     + lax.dot_general(qr[j], new_token_block(krnew_ref, j, QK_ROPE), NT_DIMS,
                                       preferred_element_type=F32))
            lane_bufs[j][4][...] = jnp.where(k_col <= t_row, s_new, -jnp.inf)
            arrive(j, 0)
        for j in range(lanes):
            stage(j, 0)
        m_lanes = [None] * lanes
        for g in range(n_groups):
            for j in range(lanes):
                refill(j, g)
            if g + 1 < n_groups:
                for j in range(lanes):
                    arrive(j, g + 1)
            for j in range(lanes):
                s_all = lane_bufs[j][3]
                cbf, krbf = lane_bufs[j][5:5 + n_groups], lane_bufs[j][5 + n_groups:5 + 2 * n_groups]
                s = lax.dot_general(qa[j], cbf[g][...], NT_DIMS, preferred_element_type=F32) + _mm(qr[j], krbf[g][...])
                s_all[:, g * gk:(g + 1) * gk] = s
                for k in range(gk // LANES):
                    blk = s[:, k * LANES:(k + 1) * LANES]
                    m_lanes[j] = blk if m_lanes[j] is None else jnp.maximum(m_lanes[j], blk)
                if g + 1 < n_groups:
                    stage(j, g + 1)
        for j in range(lanes):
            lane_bufs[j][5 + 2 * n_groups][...] = m_lanes[j]

    def phase2():
        m, l_new, l_lanes, acc = [], [], [None] * lanes, []
        for j in range(lanes):
            sn = lane_bufs[j][4][...]
            m_part = lane_bufs[j][5 + 2 * n_groups][...]
            m.append(jnp.maximum(jnp.max(sn, axis=-1, keepdims=True), jnp.max(m_part, axis=-1, keepdims=True)))
            p = jnp.exp(sn - m[j])
            l_new.append(jnp.sum(p, axis=-1, keepdims=True))
            acc.append(_mm(p.astype(BF16), new_token_block(cnew_ref, j, KV_LORA)))
        units = [(g, j) for g in range(n_groups) for j in range(lanes)]
        pending = None
        for n in range(len(units) + 1):
            if n < len(units):
                g, j = units[n]
                p = jnp.exp(lane_bufs[j][3][:, g * gk:(g + 1) * gk] - m[j])
                for k in range(gk // LANES):
                    blk = p[:, k * LANES:(k + 1) * LANES]
                    l_lanes[j] = blk if l_lanes[j] is None else l_lanes[j] + blk
                nxt = (g, j, p.astype(BF16))
            else:
                nxt = None
            if pending is not None:
                g0, j0, p0 = pending
                acc[j0] = acc[j0] + _mm(p0, lane_bufs[j0][5 + g0][...])
            pending = nxt
        for j in range(lanes):
            l = l_new[j] + jnp.sum(l_lanes[j], axis=-1, keepdims=True)
            olat_ref[j] = acc[j] * (1.0 / l)

    pl.when(b >= 0)(phase1)
    pl.when(b < n_steps)(phase2)


def _decode_attention(page_table, qabs, qr, c_new, kr_new, cache_lat, cache_rope_t, *, group=16, n_slots=4, lanes=2):
    nb, n_pages = page_table.shape
    page = cache_lat.shape[1]
    rows = N_HEADS * SUBLANES
    n_groups = n_pages // group
    assert n_pages % group == 0 and n_groups % n_slots == 0 and nb % lanes == 0
    per_step = lambda w, r: pl.BlockSpec((lanes, r, w), lambda b, pt: (b, 0, 0))
    lane_scratch = ([pltpu.VMEM((n_slots, group * page, KV_LORA), F32),
                     pltpu.VMEM((n_slots, QK_ROPE, group * page), F32),
                     pltpu.SemaphoreType.DMA((2, n_slots)),
                     pltpu.VMEM((rows, n_pages * page), F32),
                     pltpu.VMEM((rows, 2 * SUBLANES), F32)]
                    + [pltpu.VMEM((group * page, KV_LORA), BF16)] * n_groups
                    + [pltpu.VMEM((QK_ROPE, group * page), BF16)] * n_groups
                    + [pltpu.VMEM((rows, LANES), F32)])
    grid_spec = pltpu.PrefetchScalarGridSpec(
        num_scalar_prefetch=1,
        grid=(nb // lanes,),
        in_specs=[per_step(KV_LORA, rows), per_step(QK_ROPE, rows), per_step(KV_LORA, SUBLANES),
                  per_step(QK_ROPE, SUBLANES),
                  pl.BlockSpec(memory_space=pl.ANY), pl.BlockSpec(memory_space=pl.ANY)],
        out_specs=per_step(KV_LORA, rows),
        scratch_shapes=lane_scratch * lanes,
    )
    return pl.pallas_call(
        functools.partial(_decode_attn_kernel, n_pages=n_pages, group=group, page=page, n_slots=n_slots,
                          n_steps=nb // lanes, lanes=lanes),
        grid_spec=grid_spec,
        out_shape=jax.ShapeDtypeStruct((nb, rows, KV_LORA), F32),
        compiler_params=_params(),
        name="decode_attention",
    )(page_table, qabs, qr, c_new, kr_new, cache_lat, cache_rope_t)


def _uv_kernel(olat_ref, wuv_ref, o_ref):
    n = olat_ref.shape[0] * SUBLANES
    for hd in range(N_HEADS):
        x = olat_ref[:, hd].reshape(n, KV_LORA).astype(BF16)
        o_ref[:, hd * V_HEAD:(hd + 1) * V_HEAD] = _mm(x, wuv_ref[hd]).astype(BF16)


def _uv_project(olat, wuv_heads):
    nb = olat.shape[0]
    return pl.pallas_call(
        _uv_kernel,
        grid=(1,),
        in_specs=[pl.BlockSpec(olat.shape, lambda i: (0, 0, 0, 0)),
                  pl.BlockSpec(wuv_heads.shape, lambda i: (0, 0, 0))],
        out_specs=pl.BlockSpec((nb * SUBLANES, N_HEADS * V_HEAD), lambda i: (0, 0)),
        out_shape=jax.ShapeDtypeStruct((nb * SUBLANES, N_HEADS * V_HEAD), BF16),
        compiler_params=_params(),
        name="uv_project",
    )(olat, wuv_heads)


def _mixer_out_kernel(x_ref, sh_ref, sc_ref, g1_ref, o_ref, cb_ref, wg_ref, woa_ref, woc_ref, wo_ref,
                      lng_ref, lnb_ref, x1_ref, *, tm, sub):
    def front(r):
        h = (x_ref[r, :] * (1.0 + _rows(sc_ref, r, tm)) + _rows(sh_ref, r, tm)).astype(BF16)
        return _mm(h, wg_ref[...]), _mm(o_ref[r, :], woa_ref[...]), _mm(cb_ref[r, :], woc_ref[...])

    def back(r, gates, y_a, y_c):
        mixed = _sigmoid(gates[:, :D_MODEL]) * y_a + _sigmoid(gates[:, D_MODEL:]) * y_c
        mix = _mm(mixed.astype(BF16), wo_ref[...])
        x1_ref[r, :] = _layer_norm(ALPHA * x_ref[r, :] + _rows(g1_ref, r, tm) * mix, lng_ref[...], lnb_ref[...])

    _pipelined(tm, sub, front, back)


def _mixer_out(x2d, sh, sc, g1, o, cb, wts, *, prompt, tm, sub, tiles_per_seq):
    ntok = x2d.shape[0]
    row = lambda w: pl.BlockSpec((tm, w), lambda i: (i, 0))
    ada_spec = (pl.BlockSpec((None, 1, D_MODEL), lambda i: (i // tiles_per_seq, 0, 0)) if prompt else row(D_MODEL))
    return pl.pallas_call(
        functools.partial(_mixer_out_kernel, tm=tm, sub=sub),
        grid=(ntok // tm,),
        in_specs=[row(D_MODEL), ada_spec, ada_spec, ada_spec, row(N_HEADS * V_HEAD), row(CONV_DIM)]
                 + [_const_spec(w.shape) for w in wts],
        out_specs=row(D_MODEL),
        out_shape=jax.ShapeDtypeStruct((ntok, D_MODEL), F32),
        compiler_params=_params(),
        name="mixer_out_prompt" if prompt else "mixer_out_sample",
    )(x2d, sh, sc, g1, o, cb, *wts)


def _ffn_kernel(x1_ref, sh_ref, sc_ref, g2_ref, w1_ref, w3_ref, w2_ref, lng_ref, lnb_ref, y_ref, *, tm, sub):
    def front(r):
        h2 = (x1_ref[r, :] * (1.0 + _rows(sc_ref, r, tm)) + _rows(sh_ref, r, tm)).astype(BF16)
        return _mm(h2, w1_ref[...]), _mm(h2, w3_ref[...])

    def back(r, a, g):
        gated = (a * _sigmoid(a)) * g
        f = _mm(gated.astype(BF16), w2_ref[...])
        y_ref[r, :] = _layer_norm(ALPHA * x1_ref[r, :] + _rows(g2_ref, r, tm) * f, lng_ref[...], lnb_ref[...])

    _pipelined(tm, sub, front, back)


def _ffn(x1, sh, sc, g2, wts, *, prompt, tm, sub, tiles_per_seq):
    ntok = x1.shape[0]
    row = lambda w: pl.BlockSpec((tm, w), lambda i: (i, 0))
    ada_spec = (pl.BlockSpec((None, 1, D_MODEL), lambda i: (i // tiles_per_seq, 0, 0)) if prompt else row(D_MODEL))
    return pl.pallas_call(
        functools.partial(_ffn_kernel, tm=tm, sub=sub),
        grid=(ntok // tm,),
        in_specs=[row(D_MODEL), ada_spec, ada_spec, ada_spec] + [_const_spec(w.shape) for w in wts],
        out_specs=row(D_MODEL),
        out_shape=jax.ShapeDtypeStruct((ntok, D_MODEL), F32),
        compiler_params=_params(),
        name="ffn_prompt" if prompt else "ffn_sample",
    )(x1, sh, sc, g2, *wts)


def _rope_tables(pos, scale_q):
    inv = ROPE_THETA ** (-jnp.arange(0, QK_ROPE, 2, dtype=F32) / QK_ROPE)
    ang = pos.astype(F32)[:, None] * inv[None, :]
    cos, sin = jnp.cos(ang), jnp.sin(ang)
    n = pos.shape[0]
    z_lo = jnp.zeros((n, ROPE_LO), F32)
    z_hi = jnp.zeros((n, LANES - ROPE_LO - QK_ROPE), F32)
    cos_k = jnp.concatenate([z_lo, cos, cos, z_hi], axis=1)
    sin_k = jnp.concatenate([z_lo, -sin, sin, z_hi], axis=1)
    cos_q = jnp.concatenate([jnp.ones((n, ROPE_LO), F32), cos, cos, z_hi], axis=1) * scale_q
    sin_q = sin_k * scale_q
    return cos_q, sin_q, cos_k, sin_k


def _head_groups(w, width):
    k = w.shape[0]
    return jnp.pad(w, ((0, 0), (0, 0), (0, HEAD_SLAB - width))).reshape(k, N_HEADS * HEAD_SLAB)


def kernel(x_prompt, x_sample, cache_kv_latent, cache_k_rope, state_conv, page_table, c_prompt, c_sample,
           w_ada, b_ada, w_in, q_norm_g, kv_norm_g, w_uq, w_ukv, w_oa, conv_w, w_oc, w_o,
           ln1_g, ln1_b, w_ff1, w_ff3, w_ff2, ln2_g, ln2_b):
    assert w_ada.shape[0] == DEPTH == 1
    nb_p, seq_p, _ = x_prompt.shape
    nb_s, seq_s, _ = x_sample.shape
    assert seq_s == SUBLANES
    n_pool, page = cache_kv_latent.shape[1], cache_kv_latent.shape[2]
    past_len = page_table.shape[1] * page
    layer = 0

    s0, s1, s2, s3 = Q_LORA, Q_LORA + KV_LORA, Q_LORA + KV_LORA + QK_ROPE, Q_LORA + KV_LORA + QK_ROPE + 3 * CONV_DIM
    w_in_l = w_in[layer]
    w_kr_group = jnp.pad(w_in_l[:, s1:s2], ((0, 0), (ROPE_LO, LANES - ROPE_LO - QK_ROPE)))
    wa = jnp.concatenate([w_in_l[:, :s1], w_kr_group], axis=1).astype(BF16)
    wb = w_in_l[:, s2:s3].astype(BF16)
    wg = w_in_l[:, s3:].astype(BF16)
    wuq = _head_groups(w_uq[layer].reshape(Q_LORA, N_HEADS, QK_HEAD), QK_HEAD).astype(BF16)
    w_uk = w_ukv[layer][:, :, :QK_NOPE]
    w_uv = w_ukv[layer][:, :, QK_NOPE:]
    wuk_groups = _head_groups(w_uk, QK_NOPE).astype(BF16)
    wuv_t = w_uv.reshape(KV_LORA, N_HEADS * V_HEAD).T.astype(BF16)
    wuv_heads = w_uv.transpose(1, 0, 2).astype(BF16)
    wukt = jnp.pad(w_uk.transpose(1, 2, 0), ((0, 0), (0, HEAD_SLAB - QK_NOPE), (0, 0))).astype(BF16)
    qg = q_norm_g[layer][None, :]
    kvg = kv_norm_g[layer][None, :]
    convw = conv_w[layer]
    in_wts = (wa, wb, qg, kvg, wuq, convw)
    out_wts = (wg, w_oa[layer].astype(BF16), w_oc[layer].astype(BF16), w_o[layer].astype(BF16),
               ln1_g[layer][None, :], ln1_b[layer][None, :])
    ffn_wts = (w_ff1[layer].astype(BF16), w_ff3[layer].astype(BF16), w_ff2[layer].astype(BF16),
               ln2_g[layer][None, :], ln2_b[layer][None, :])

    ada = _ada_terms(jnp.concatenate([c_prompt, c_sample], axis=0), w_ada[layer].astype(BF16), b_ada[layer][None, :])
    ada_p = [a[:, None, :] for a in jnp.split(ada[:nb_p], 6, axis=-1)]
    ada_s = [jnp.repeat(a, seq_s, axis=0) for a in jnp.split(ada[nb_p:], 6, axis=-1)]

    tm_p = 1024
    tps = seq_p // tm_p
    xp = x_prompt.reshape(nb_p * seq_p, D_MODEL)
    tabs_p = _rope_tables(jnp.arange(seq_p), ATTN_SCALE * math.log2(math.e))
    ckv_p, kr_p, cb_p, u_p, q_p, k_p, vt_p = _mixer_in(
        xp, ada_p[0], ada_p[1], in_wts, tabs_p, prompt=True, nb=nb_p, seq=seq_p, tm=1024, sub=256,
        extra=(wuk_groups, wuv_t))
    o_p = _prompt_attention(q_p, k_p, vt_p, nb=nb_p, seq=seq_p)
    x1_p = _mixer_out(xp, ada_p[0], ada_p[1], ada_p[2], o_p.reshape(nb_p * seq_p, N_HEADS * V_HEAD), cb_p, out_wts,
                      prompt=True, tm=tm_p, sub=256, tiles_per_seq=tps)
    y_p = _ffn(x1_p, ada_p[3], ada_p[4], ada_p[5], ffn_wts, prompt=True, tm=tm_p, sub=256, tiles_per_seq=tps)

    tm_s = 256
    ntok_s = nb_s * seq_s
    xs = x_sample.reshape(ntok_s, D_MODEL)
    tabs_s = tuple(jnp.tile(t, (tm_s // seq_s, 1)) for t in _rope_tables(past_len + jnp.arange(seq_s), ATTN_SCALE))
    st = state_conv[layer]
    zrow = jnp.zeros((nb_s, seq_s - 1, CONV_DIM), F32)
    fix1 = jnp.concatenate([st[:, 1:2], zrow], axis=1).reshape(ntok_s, CONV_DIM)
    fix2 = jnp.concatenate([st, zrow[:, 1:]], axis=1).reshape(ntok_s, CONV_DIM)
    ckv_s, kr_s, cb_s, u_s, qabs_s, qr_s = _mixer_in(
        xs, ada_s[0], ada_s[1], in_wts, tabs_s, prompt=False, nb=nb_s, seq=seq_s, tm=tm_s, sub=tm_s,
        extra=(wukt, fix1, fix2))
    rows = N_HEADS * seq_s
    olat = _decode_attention(
        page_table, qabs_s.reshape(nb_s, rows, KV_LORA), qr_s.reshape(nb_s, rows, QK_ROPE),
        ckv_s.reshape(nb_s, seq_s, KV_LORA), kr_s.reshape(nb_s, seq_s, QK_ROPE),
        cache_kv_latent.reshape(n_pool, page, KV_LORA),
        jnp.swapaxes(cache_k_rope.reshape(n_pool, page, QK_ROPE), 1, 2))
    o_s = _uv_project(olat.reshape(nb_s, N_HEADS, seq_s, KV_LORA), wuv_heads)
    x1_s = _mixer_out(xs, ada_s[0], ada_s[1], ada_s[2], o_s, cb_s, out_wts, prompt=False, tm=tm_s, sub=tm_s, tiles_per_seq=1)
    y_s = _ffn(x1_s, ada_s[3], ada_s[4], ada_s[5], ffn_wts, prompt=False, tm=tm_s, sub=tm_s, tiles_per_seq=1)

    hist = CONV_WIDTH - 1
    return (y_p.reshape(nb_p, seq_p, D_MODEL),
            y_s.reshape(nb_s, seq_s, D_MODEL),
            ckv_p.reshape(1, nb_p, seq_p, KV_LORA),
            kr_p.reshape(1, nb_p, seq_p, QK_ROPE),
            u_p.reshape(1, nb_p, SUBLANES, CONV_DIM)[:, :, SUBLANES - hist:],
            ckv_s.reshape(1, nb_s, seq_s, KV_LORA),
            kr_s.reshape(1, nb_s, seq_s, QK_ROPE),
            u_s.reshape(1, nb_s, seq_s, CONV_DIM)[:, :, seq_s - hist:])
```

```python
import functools
import math

import jax
import jax.numpy as jnp
from jax import lax
from jax.experimental import pallas as pl
from jax.experimental.pallas import tpu as pltpu

D_MODEL = 1024
N_HEADS = 8
QK_NOPE = 64
QK_ROPE = 32
QK_HEAD = QK_NOPE + QK_ROPE
V_HEAD = 64
Q_LORA = 384
KV_LORA = 256
ROPE_THETA = 10000.0
ATTN_SCALE = QK_HEAD ** -0.5
CONV_DIM = 512
CONV_WIDTH = 3
D_FF = int(math.ceil(8 * D_MODEL / 3 / 256)) * 256
DEPTH = 1
ALPHA = (2.0 * DEPTH) ** 0.25
LN_EPS = 1e-5
RMS_EPS = 1e-6

LANES = 128
SUBLANES = 8
HEAD_SLAB = LANES
ROPE_LO = QK_NOPE
ROPE_HALF = QK_ROPE // 2
VMEM_LIMIT = 52 * 1024 * 1024

BF16 = jnp.bfloat16
F32 = jnp.float32
NT_DIMS = (((1,), (1,)), ((), ()))


def _params(n_axes=1, vmem=VMEM_LIMIT):
    return pltpu.CompilerParams(dimension_semantics=("arbitrary",) * n_axes, vmem_limit_bytes=vmem)


def _const_spec(shape):
    nd = len(shape)
    return pl.BlockSpec(shape, lambda *_: (0,) * nd, pipeline_mode=pl.Buffered(1))


def _mm(a, b):
    return jnp.dot(a, b, preferred_element_type=F32)


def _sigmoid(x):
    return 1.0 / (1.0 + jnp.exp(-x))


def _layer_norm(x, g, b):
    mu = jnp.mean(x, axis=-1, keepdims=True)
    xc = x - mu
    var = jnp.mean(xc * xc, axis=-1, keepdims=True)
    return xc * lax.rsqrt(var + LN_EPS) * g + b


def _rms_norm(x, g):
    return x * lax.rsqrt(jnp.mean(x * x, axis=-1, keepdims=True) + RMS_EPS) * g


def _rows(ref, r, tm):
    return ref[r, :] if ref.shape[0] == tm else ref[...]


def _pipelined(tm, sub, front, back):
    subs = [slice(r0, r0 + sub) for r0 in range(0, tm, sub)]
    pending = front(subs[0])
    for n, r in enumerate(subs):
        nxt = front(subs[n + 1]) if n + 1 < len(subs) else None
        back(r, *pending)
        pending = nxt


def _rope_slab(x, cos_t, sin_t):
    lane = lax.broadcasted_iota(jnp.int32, x.shape, 1)
    swapped = jnp.where(lane < ROPE_LO + ROPE_HALF,
                        pltpu.roll(x, LANES - ROPE_HALF, 1), pltpu.roll(x, ROPE_HALF, 1))
    return x * cos_t + swapped * sin_t


def _ada_kernel(c_ref, w_ref, b_ref, o_ref):
    c = c_ref[...]
    s = c * _sigmoid(c)
    o_ref[...] = _mm(s.astype(BF16), w_ref[...]) + b_ref[...]


def _ada_terms(c, w_ada, b_ada):
    n, tn = c.shape[0], 1536
    return pl.pallas_call(
        _ada_kernel,
        grid=(6 * D_MODEL // tn,),
        in_specs=[pl.BlockSpec((n, D_MODEL), lambda j: (0, 0)),
                  pl.BlockSpec((D_MODEL, tn), lambda j: (0, j)),
                  pl.BlockSpec((1, tn), lambda j: (0, j))],
        out_specs=pl.BlockSpec((n, tn), lambda j: (0, j)),
        out_shape=jax.ShapeDtypeStruct((n, 6 * D_MODEL), F32),
        compiler_params=_params(),
        name="ada_terms",
    )(c, w_ada, b_ada)


def _mixer_in_kernel(*refs, prompt, tm, sub, tiles_per_seq):
    if prompt:
        (x_ref, sh_ref, sc_ref, wa_ref, wb_ref, qg_ref, kvg_ref, wuq_ref, convw_ref,
         cq_ref, sq_ref, ck_ref, sk_ref, wuk_ref, wuvt_ref,
         ckv_ref, kr_ref, cb_ref, u_ref, q_ref, k_ref, vt_ref, ubuf) = refs
    else:
        (x_ref, sh_ref, sc_ref, wa_ref, wb_ref, qg_ref, kvg_ref, wuq_ref, convw_ref,
         cq_ref, sq_ref, ck_ref, sk_ref, wukt_ref, fix1_ref, fix2_ref,
         ckv_ref, kr_ref, cb_ref, u_ref, qabs_ref, qr_ref, ubuf) = refs
    i = pl.program_id(0)

    @pl.when(i % tiles_per_seq == 0)
    def _():
        ubuf[0:SUBLANES, :] = jnp.zeros((SUBLANES, CONV_DIM), F32)

    def front(r):
        h = (x_ref[r, :] * (1.0 + _rows(sc_ref, r, tm)) + _rows(sh_ref, r, tm)).astype(BF16)
        return _mm(h, wa_ref[...]), _mm(h, wb_ref[...])

    def back(r, za, zb):
        r0, n = r.start, r.stop - r.start
        q_lat = za[:, :Q_LORA]
        kv_lat = za[:, Q_LORA:Q_LORA + KV_LORA]
        kr_slab = za[:, Q_LORA + KV_LORA:]
        c_kv = _rms_norm(kv_lat, kvg_ref[...])
        ckv_ref[r, :] = c_kv
        kr_rot = _rope_slab(kr_slab, ck_ref[r, :], sk_ref[r, :])
        kr_ref[r, :] = kr_rot[:, ROPE_LO:ROPE_LO + QK_ROPE]

        q_all = _mm(_rms_norm(q_lat, qg_ref[...]).astype(BF16), wuq_ref[...])
        cq, sq = cq_ref[r, :], sq_ref[r, :]
        q_heads = [_rope_slab(q_all[:, hd * HEAD_SLAB:(hd + 1) * HEAD_SLAB], cq, sq) for hd in range(N_HEADS)]

        b_g = zb[:, :CONV_DIM]
        u = zb[:, CONV_DIM:2 * CONV_DIM] * zb[:, 2 * CONV_DIM:]
        ubuf[SUBLANES + r0:SUBLANES + r0 + n, :] = u
        u1 = ubuf[pl.ds(SUBLANES + r0 - 1, n), :]
        u2 = ubuf[pl.ds(SUBLANES + r0 - 2, n), :]
        if prompt:
            if r.stop == tm:
                ubuf[0:SUBLANES, :] = u[n - SUBLANES:, :]
                u_ref[...] = u[n - SUBLANES:, :]
        else:
            t = lax.broadcasted_iota(jnp.int32, (n, CONV_DIM), 0) % SUBLANES
            u1 = jnp.where(t == 0, fix1_ref[r, :], u1)
            u2 = jnp.where(t < 2, fix2_ref[r, :], u2)
            u_ref[r, :] = u
        cw = convw_ref[...]
        conv_y = u2 * cw[0:1, :] + u1 * cw[1:2, :] + u * cw[2:3, :]
        cb_ref[r, :] = (b_g * conv_y).astype(BF16)

        if prompt:
            c_bf = c_kv.astype(BF16)
            k_nope = _mm(c_bf, wuk_ref[...])
            vt_ref[:, r] = lax.dot_general(wuvt_ref[...], c_bf, NT_DIMS, preferred_element_type=F32).astype(BF16)
            for hd in range(N_HEADS):
                q_ref[hd, r, :] = q_heads[hd].astype(BF16)
                k_ref[hd, r, :] = (k_nope[:, hd * HEAD_SLAB:(hd + 1) * HEAD_SLAB] + kr_rot).astype(BF16)
        else:
            seqs = slice(r0 // SUBLANES, r.stop // SUBLANES)
            for hd in range(N_HEADS):
                qh = q_heads[hd]
                qabs = _mm(qh.astype(BF16), wukt_ref[hd])
                qabs_ref[seqs, hd] = qabs.reshape(n // SUBLANES, SUBLANES, KV_LORA)
                qr_ref[seqs, hd] = qh[:, ROPE_LO:ROPE_LO + QK_ROPE].reshape(n // SUBLANES, SUBLANES, QK_ROPE)

    _pipelined(tm, sub, front, back)


def _mixer_in(x2d, sh, sc, wts, tabs, *, prompt, nb, seq, tm, sub, extra):
    ntok = nb * seq
    tiles_per_seq = seq // tm if prompt else 1
    grid = (ntok // tm,)
    row = lambda w: pl.BlockSpec((tm, w), lambda i: (i, 0))
    if prompt:
        ada_spec = pl.BlockSpec((None, 1, D_MODEL), lambda i: (i // tiles_per_seq, 0, 0))
        tab_spec = pl.BlockSpec((tm, LANES), lambda i: (i % tiles_per_seq, 0))
    else:
        ada_spec = row(D_MODEL)
        tab_spec = pl.BlockSpec((tm, LANES), lambda i: (0, 0))
    wa, wb, qg, kvg, wuq, convw = wts
    in_specs = [row(D_MODEL), ada_spec, ada_spec,
                _const_spec(wa.shape), _const_spec(wb.shape), _const_spec(qg.shape), _const_spec(kvg.shape),
                _const_spec(wuq.shape), _const_spec(convw.shape),
                tab_spec, tab_spec, tab_spec, tab_spec]
    out_shape = [jax.ShapeDtypeStruct((ntok, KV_LORA), F32), jax.ShapeDtypeStruct((ntok, QK_ROPE), F32),
                 jax.ShapeDtypeStruct((ntok, CONV_DIM), BF16)]
    out_specs = [row(KV_LORA), row(QK_ROPE), row(CONV_DIM)]
    if prompt:
        wuk, wuv = extra
        in_specs += [_const_spec(wuk.shape), _const_spec(wuv.shape)]
        head_spec = pl.BlockSpec((None, N_HEADS, tm, HEAD_SLAB),
                                 lambda i: (i // tiles_per_seq, 0, i % tiles_per_seq, 0))
        out_shape += [jax.ShapeDtypeStruct((nb * SUBLANES, CONV_DIM), F32),
                      jax.ShapeDtypeStruct((nb, N_HEADS, seq, HEAD_SLAB), BF16),
                      jax.ShapeDtypeStruct((nb, N_HEADS, seq, HEAD_SLAB), BF16),
                      jax.ShapeDtypeStruct((nb, N_HEADS * V_HEAD, seq), BF16)]
        out_specs += [pl.BlockSpec((SUBLANES, CONV_DIM), lambda i: (i // tiles_per_seq, 0)),
                      head_spec, head_spec,
                      pl.BlockSpec((None, N_HEADS * V_HEAD, tm), lambda i: (i // tiles_per_seq, 0, i % tiles_per_seq))]
    else:
        wukt, fix1, fix2 = extra
        in_specs += [_const_spec(wukt.shape), row(CONV_DIM), row(CONV_DIM)]
        nseq = tm // SUBLANES
        out_shape += [jax.ShapeDtypeStruct((ntok, CONV_DIM), F32),
                      jax.ShapeDtypeStruct((nb, N_HEADS, SUBLANES, KV_LORA), F32),
                      jax.ShapeDtypeStruct((nb, N_HEADS, SUBLANES, QK_ROPE), F32)]
        out_specs += [row(CONV_DIM),
                      pl.BlockSpec((nseq, N_HEADS, SUBLANES, KV_LORA), lambda i: (i, 0, 0, 0)),
                      pl.BlockSpec((nseq, N_HEADS, SUBLANES, QK_ROPE), lambda i: (i, 0, 0, 0))]
    return pl.pallas_call(
        functools.partial(_mixer_in_kernel, prompt=prompt, tm=tm, sub=sub, tiles_per_seq=tiles_per_seq),
        grid=grid, in_specs=in_specs, out_specs=out_specs, out_shape=out_shape,
        scratch_shapes=[pltpu.VMEM((SUBLANES + tm, CONV_DIM), F32)],
        compiler_params=_params(),
        name="mixer_in_prompt" if prompt else "mixer_in_sample",
    )(x2d, sh, sc, wa, wb, qg, kvg, wuq, convw, *tabs, *extra)


def _prompt_attn_kernel(q_ref, k_ref, vt_ref, o_ref, *, seq, tq, heads):
    th = tq // 2
    key_le_query = (lax.broadcasted_iota(jnp.int32, (th, th), 0) <= lax.broadcasted_iota(jnp.int32, (th, th), 1))

    def scores(i, hh):
        lo, mid, hi = i * tq, i * tq + th, (i + 1) * tq
        q = q_ref[hh, lo:hi, :]
        s_aa = lax.dot_general(k_ref[hh, lo:mid, :], q[0:th], NT_DIMS, preferred_element_type=F32)
        s_b = lax.dot_general(k_ref[hh, lo:hi, :], q[th:tq], NT_DIMS, preferred_element_type=F32)
        s_aa = jnp.where(key_le_query, s_aa, -jnp.inf)
        s_bb = jnp.where(key_le_query, s_b[th:tq], -jnp.inf)
        so = lax.dot_general(k_ref[hh, 0:lo, :], q, NT_DIMS, preferred_element_type=F32) if i > 0 else None
        return s_aa, s_b[0:th], s_bb, so

    def softmax_pv(i, hh, s_aa, s_ab, s_bb, so):
        lo, mid, hi = i * tq, i * tq + th, (i + 1) * tq
        vt = vt_ref.at[hh * V_HEAD:(hh + 1) * V_HEAD, :]
        m_a = jnp.max(s_aa, axis=0, keepdims=True)
        m_b = jnp.maximum(jnp.max(s_ab, axis=0, keepdims=True), jnp.max(s_bb, axis=0, keepdims=True))
        m = jnp.concatenate([m_a, m_b], axis=1)
        if so is not None:
            m = jnp.maximum(m, jnp.max(so, axis=0, keepdims=True))
        m_a, m_b = m[:, 0:th], m[:, th:tq]
        p_aa = jnp.exp2(s_aa - m_a)
        p_ab = jnp.exp2(s_ab - m_b)
        p_bb = jnp.exp2(s_bb - m_b)
        l = jnp.concatenate([jnp.sum(p_aa, axis=0, keepdims=True),
                             jnp.sum(p_ab, axis=0, keepdims=True) + jnp.sum(p_bb, axis=0, keepdims=True)], axis=1)
        p_b = jnp.concatenate([p_ab, p_bb], axis=0).astype(BF16)
        acc = jnp.concatenate([_mm(vt[:, lo:mid], p_aa.astype(BF16)), _mm(vt[:, lo:hi], p_b)], axis=1)
        if so is not None:
            po = jnp.exp2(so - m)
            l = l + jnp.sum(po, axis=0, keepdims=True)
            acc = acc + _mm(vt[:, 0:lo], po.astype(BF16))
        return acc * (1.0 / l)

    n_tiles = seq // tq
    units = [(i, hh) for i in reversed(range(n_tiles)) for hh in range(heads)]
    pending = scores(*units[0])
    outs = []
    for n, (i, hh) in enumerate(units):
        nxt = scores(*units[n + 1]) if n + 1 < len(units) else None
        outs.append(softmax_pv(i, hh, *pending))
        pending = nxt
        if hh % 2 == 1:
            o_ref[i * tq:(i + 1) * tq, (hh // 2) * LANES:(hh // 2 + 1) * LANES] = (
                jnp.concatenate(outs, axis=0).T.astype(BF16))
            outs = []


def _prompt_attention(q, k, vt, *, nb, seq, tq=512, heads=4):
    head_spec = pl.BlockSpec((None, heads, seq, HEAD_SLAB), lambda b, p: (b, p, 0, 0))
    return pl.pallas_call(
        functools.partial(_prompt_attn_kernel, seq=seq, tq=tq, heads=heads),
        grid=(nb, N_HEADS // heads),
        in_specs=[head_spec, head_spec, pl.BlockSpec((None, heads * V_HEAD, seq), lambda b, p: (b, p, 0))],
        out_specs=pl.BlockSpec((None, seq, heads * V_HEAD), lambda b, p: (b, 0, p)),
        out_shape=jax.ShapeDtypeStruct((nb, seq, N_HEADS * V_HEAD), BF16),
        compiler_params=_params(2),
        name="prompt_attention",
    )(q, k, vt)


def _decode_attn_kernel(pt_ref, qabs_ref, qr_ref, cnew_ref, krnew_ref, lat_hbm, ropet_hbm, olat_ref, *bufs,
                        n_pages, group, page, n_slots, n_steps, lanes):
    b = pl.program_id(0)
    n_groups = n_pages // group
    gk = group * page
    rows = N_HEADS * SUBLANES
    per_lane = 6 + 2 * n_groups
    lane_bufs = [bufs[j * per_lane:(j + 1) * per_lane] for j in range(lanes)]

    def page_copies(j, step, g):
        latbuf, ropebuf, sems = lane_bufs[j][:3]
        slot = g % n_slots
        out = []
        for p in range(group):
            pid = pt_ref[step * lanes + j, g * group + p]
            out.append(pltpu.make_async_copy(lat_hbm.at[pid], latbuf.at[slot, pl.ds(p * page, page)], sems.at[0, slot]))
            out.append(pltpu.make_async_copy(ropet_hbm.at[pid], ropebuf.at[slot, :, pl.ds(p * page, page)], sems.at[1, slot]))
        return out

    @pl.when(b == 0)
    def _():
        for g in range(n_slots):
            for j in range(lanes):
                for cp in page_copies(j, 0, g):
                    cp.start()

    def new_token_block(ref, j, width):
        return jnp.concatenate([ref[j], jnp.zeros((SUBLANES, width), F32)], axis=0).astype(BF16)

    def arrive(j, g):
        for cp in page_copies(j, b, g):
            cp.wait()

    def stage(j, g):
        latbuf, ropebuf = lane_bufs[j][:2]
        cbf, krbf = lane_bufs[j][5:5 + n_groups], lane_bufs[j][5 + n_groups:5 + 2 * n_groups]
        cbf[g][...] = latbuf[g % n_slots].astype(BF16)
        krbf[g][...] = ropebuf[g % n_slots].astype(BF16)

    def refill(j, g):
        nxt = g + n_slots
        if nxt < n_groups:
            for cp in page_copies(j, b, nxt):
                cp.start()
        else:
            @pl.when(b + 1 < n_steps)
            def _():
                for cp in page_copies(j, b + 1, nxt - n_groups):
                    cp.start()

    def phase1():
        qa = [qabs_ref[j].astype(BF16) for j in range(lanes)]
        qr = [qr_ref[j].astype(BF16) for j in range(lanes)]
        t_row = lax.broadcasted_iota(jnp.int32, (rows, 2 * SUBLANES), 0) % SUBLANES
        k_col = lax.broadcasted_iota(jnp.int32, (rows, 2 * SUBLANES), 1)
        for j in range(lanes):
            s_new = (lax.dot_general(qa[j], new_token_block(cnew_ref, j, KV_LORA), NT_DIMS, preferred_element_type=F32)
                     + lax.dot_general(qr[j], new_token_block(krnew_ref, j, QK_ROPE), NT_DIMS,
                                       preferred_element_type=F32))
            lane_bufs[j][4][...] = jnp.where(k_col <= t_row, s_new, -jnp.inf)
            arrive(j, 0)
        for j in range(lanes):
            stage(j, 0)
        m_lanes = [None] * lanes
        for g in range(n_groups):
            for j in range(lanes):
                refill(j, g)
            if g + 1 < n_groups:
                for j in range(lanes):
                    arrive(j, g + 1)
            for j in range(lanes):
                s_all = lane_bufs[j][3]
                cbf, krbf = lane_bufs[j][5:5 + n_groups], lane_bufs[j][5 + n_groups:5 + 2 * n_groups]
                s = lax.dot_general(qa[j], cbf[g][...], NT_DIMS, preferred_element_type=F32) + _mm(qr[j], krbf[g][...])
                s_all[:, g * gk:(g + 1) * gk] = s
                for k in range(gk // LANES):
                    blk = s[:, k * LANES:(k + 1) * LANES]
                    m_lanes[j] = blk if m_lanes[j] is None else jnp.maximum(m_lanes[j], blk)
                if g + 1 < n_groups:
                    stage(j, g + 1)
        for j in range(lanes):
            lane_bufs[j][5 + 2 * n_groups][...] = m_lanes[j]

    def phase2():
        m, l_new, l_lanes, acc = [], [], [None] * lanes, []
        for j in range(lanes):
            sn = lane_bufs[j][4][...]
            m_part = lane_bufs[j][5 + 2 * n_groups][...]
            m.append(jnp.maximum(jnp.max(sn, axis=-1, keepdims=True), jnp.max(m_part, axis=-1, keepdims=True)))
            p = jnp.exp(sn - m[j])
            l_new.append(jnp.sum(p, axis=-1, keepdims=True))
            acc.append(_mm(p.astype(BF16), new_token_block(cnew_ref, j, KV_LORA)))
        units = [(g, j) for g in range(n_groups) for j in range(lanes)]
        pending = None
        for n in range(len(units) + 1):
            if n < len(units):
                g, j = units[n]
                p = jnp.exp(lane_bufs[j][3][:, g * gk:(g + 1) * gk] - m[j])
                for k in range(gk // LANES):
                    blk = p[:, k * LANES:(k + 1) * LANES]
                    l_lanes[j] = blk if l_lanes[j] is None else l_lanes[j] + blk
                nxt = (g, j, p.astype(BF16))
            else:
                nxt = None
            if pending is not None:
                g0, j0, p0 = pending
                acc[j0] = acc[j0] + _mm(p0, lane_bufs[j0][5 + g0][...])
            pending = nxt
        for j in range(lanes):
            l = l_new[j] + jnp.sum(l_lanes[j], axis=-1, keepdims=True)
            olat_ref[j] = acc[j] * (1.0 / l)

    pl.when(b >= 0)(phase1)
    pl.when(b < n_steps)(phase2)


def _decode_attention(page_table, qabs, qr, c_new, kr_new, cache_lat, cache_rope_t, *, group=16, n_slots=4, lanes=2):
    nb, n_pages = page_table.shape
    page = cache_lat.shape[1]
    rows = N_HEADS * SUBLANES
    n_groups = n_pages // group
    assert n_pages % group == 0 and n_groups % n_slots == 0 and nb % lanes == 0
    per_step = lambda w, r: pl.BlockSpec((lanes, r, w), lambda b, pt: (b, 0, 0))
    lane_scratch = ([pltpu.VMEM((n_slots, group * page, KV_LORA), F32),
                     pltpu.VMEM((n_slots, QK_ROPE, group * page), F32),
                     pltpu.SemaphoreType.DMA((2, n_slots)),
                     pltpu.VMEM((rows, n_pages * page), F32),
                     pltpu.VMEM((rows, 2 * SUBLANES), F32)]
                    + [pltpu.VMEM((group * page, KV_LORA), BF16)] * n_groups
                    + [pltpu.VMEM((QK_ROPE, group * page), BF16)] * n_groups
                    + [pltpu.VMEM((rows, LANES), F32)])
    grid_spec = pltpu.PrefetchScalarGridSpec(
        num_scalar_prefetch=1,
        grid=(nb // lanes,),
        in_specs=[per_step(KV_LORA, rows), per_step(QK_ROPE, rows), per_step(KV_LORA, SUBLANES),
                  per_step(QK_ROPE, SUBLANES),
                  pl.BlockSpec(memory_space=pl.ANY), pl.BlockSpec(memory_space=pl.ANY)],
        out_specs=per_step(KV_LORA, rows),
        scratch_shapes=lane_scratch * lanes,
    )
    return pl.pallas_call(
        functools.partial(_decode_attn_kernel, n_pages=n_pages, group=group, page=page, n_slots=n_slots,
                          n_steps=nb // lanes, lanes=lanes),
        grid_spec=grid_spec,
        out_shape=jax.ShapeDtypeStruct((nb, rows, KV_LORA), F32),
        compiler_params=_params(),
        name="decode_attention",
    )(page_table, qabs, qr, c_new, kr_new, cache_lat, cache_rope_t)


def _uv_kernel(olat_ref, wuv_ref, o_ref):
    n = olat_ref.shape[0] * SUBLANES
    for hd in range(N_HEADS):
        x = olat_ref[:, hd].reshape(n, KV_LORA).astype(BF16)
        o_ref[:, hd * V_HEAD:(hd + 1) * V_HEAD] = _mm(x, wuv_ref[hd]).astype(BF16)


def _uv_project(olat, wuv_heads):
    nb = olat.shape[0]
    return pl.pallas_call(
        _uv_kernel,
        grid=(1,),
        in_specs=[pl.BlockSpec(olat.shape, lambda i: (0, 0, 0, 0)),
                  pl.BlockSpec(wuv_heads.shape, lambda i: (0, 0, 0))],
        out_specs=pl.BlockSpec((nb * SUBLANES, N_HEADS * V_HEAD), lambda i: (0, 0)),
        out_shape=jax.ShapeDtypeStruct((nb * SUBLANES, N_HEADS * V_HEAD), BF16),
        compiler_params=_params(),
        name="uv_project",
    )(olat, wuv_heads)


def _mixer_out_kernel(x_ref, sh_ref, sc_ref, g1_ref, o_ref, cb_ref, wg_ref, woa_ref, woc_ref, wo_ref,
                      lng_ref, lnb_ref, x1_ref, *, tm, sub):
    def front(r):
        h = (x_ref[r, :] * (1.0 + _rows(sc_ref, r, tm)) + _rows(sh_ref, r, tm)).astype(BF16)
        return _mm(h, wg_ref[...]), _mm(o_ref[r, :], woa_ref[...]), _mm(cb_ref[r, :], woc_ref[...])

    def back(r, gates, y_a, y_c):
        mixed = _sigmoid(gates[:, :D_MODEL]) * y_a + _sigmoid(gates[:, D_MODEL:]) * y_c
        mix = _mm(mixed.astype(BF16), wo_ref[...])
        x1_ref[r, :] = _layer_norm(ALPHA * x_ref[r, :] + _rows(g1_ref, r, tm) * mix, lng_ref[...], lnb_ref[...])

    _pipelined(tm, sub, front, back)


def _mixer_out(x2d, sh, sc, g1, o, cb, wts, *, prompt, tm, sub, tiles_per_seq):
    ntok = x2d.shape[0]
    row = lambda w: pl.BlockSpec((tm, w), lambda i: (i, 0))
    ada_spec = (pl.BlockSpec((None, 1, D_MODEL), lambda i: (i // tiles_per_seq, 0, 0)) if prompt else row(D_MODEL))
    return pl.pallas_call(
        functools.partial(_mixer_out_kernel, tm=tm, sub=sub),
        grid=(ntok // tm,),
        in_specs=[row(D_MODEL), ada_spec, ada_spec, ada_spec, row(N_HEADS * V_HEAD), row(CONV_DIM)]
                 + [_const_spec(w.shape) for w in wts],
        out_specs=row(D_MODEL),
        out_shape=jax.ShapeDtypeStruct((ntok, D_MODEL), F32),
        compiler_params=_params(),
        name="mixer_out_prompt" if prompt else "mixer_out_sample",
    )(x2d, sh, sc, g1, o, cb, *wts)


def _ffn_kernel(x1_ref, sh_ref, sc_ref, g2_ref, w1_ref, w3_ref, w2_ref, lng_ref, lnb_ref, y_ref, *, tm, sub):
    def front(r):
        h2 = (x1_ref[r, :] * (1.0 + _rows(sc_ref, r, tm)) + _rows(sh_ref, r, tm)).astype(BF16)
        return _mm(h2, w1_ref[...]), _mm(h2, w3_ref[...])

    def back(r, a, g):
        gated = (a * _sigmoid(a)) * g
        f = _mm(gated.astype(BF16), w2_ref[...])
        y_ref[r, :] = _layer_norm(ALPHA * x1_ref[r, :] + _rows(g2_ref, r, tm) * f, lng_ref[...], lnb_ref[...])

    _pipelined(tm, sub, front, back)


def _ffn(x1, sh, sc, g2, wts, *, prompt, tm, sub, tiles_per_seq):
    ntok = x1.shape[0]
    row = lambda w: pl.BlockSpec((tm, w), lambda i: (i, 0))
    ada_spec = (pl.BlockSpec((None, 1, D_MODEL), lambda i: (i // tiles_per_seq, 0, 0)) if prompt else row(D_MODEL))
    return pl.pallas_call(
        functools.partial(_ffn_kernel, tm=tm, sub=sub),
        grid=(ntok // tm,),
        in_specs=[row(D_MODEL), ada_spec, ada_spec, ada_spec] + [_const_spec(w.shape) for w in wts],
        out_specs=row(D_MODEL),
        out_shape=jax.ShapeDtypeStruct((ntok, D_MODEL), F32),
        compiler_params=_params(),
        name="ffn_prompt" if prompt else "ffn_sample",
    )(x1, sh, sc, g2, *wts)


def _rope_tables(pos, scale_q):
    inv = ROPE_THETA ** (-jnp.arange(0, QK_ROPE, 2, dtype=F32) / QK_ROPE)
    ang = pos.astype(F32)[:, None] * inv[None, :]
    cos, sin = jnp.cos(ang), jnp.sin(ang)
    n = pos.shape[0]
    z_lo = jnp.zeros((n, ROPE_LO), F32)
    z_hi = jnp.zeros((n, LANES - ROPE_LO - QK_ROPE), F32)
    cos_k = jnp.concatenate([z_lo, cos, cos, z_hi], axis=1)
    sin_k = jnp.concatenate([z_lo, -sin, sin, z_hi], axis=1)
    cos_q = jnp.concatenate([jnp.ones((n, ROPE_LO), F32), cos, cos, z_hi], axis=1) * scale_q
    sin_q = sin_k * scale_q
    return cos_q, sin_q, cos_k, sin_k


def _head_groups(w, width):
    k = w.shape[0]
    return jnp.pad(w, ((0, 0), (0, 0), (0, HEAD_SLAB - width))).reshape(k, N_HEADS * HEAD_SLAB)


def kernel(x_prompt, x_sample, cache_kv_latent, cache_k_rope, state_conv, page_table, c_prompt, c_sample,
           w_ada, b_ada, w_in, q_norm_g, kv_norm_g, w_uq, w_ukv, w_oa, conv_w, w_oc, w_o,
           ln1_g, ln1_b, w_ff1, w_ff3, w_ff2, ln2_g, ln2_b):
    assert w_ada.shape[0] == DEPTH == 1
    nb_p, seq_p, _ = x_prompt.shape
    nb_s, seq_s, _ = x_sample.shape
    assert seq_s == SUBLANES
    n_pool, page = cache_kv_latent.shape[1], cache_kv_latent.shape[2]
    past_len = page_table.shape[1] * page
    layer = 0

    s0, s1, s2, s3 = Q_LORA, Q_LORA + KV_LORA, Q_LORA + KV_LORA + QK_ROPE, Q_LORA + KV_LORA + QK_ROPE + 3 * CONV_DIM
    w_in_l = w_in[layer]
    w_kr_group = jnp.pad(w_in_l[:, s1:s2], ((0, 0), (ROPE_LO, LANES - ROPE_LO - QK_ROPE)))
    wa = jnp.concatenate([w_in_l[:, :s1], w_kr_group], axis=1).astype(BF16)
    wb = w_in_l[:, s2:s3].astype(BF16)
    wg = w_in_l[:, s3:].astype(BF16)
    wuq = _head_groups(w_uq[layer].reshape(Q_LORA, N_HEADS, QK_HEAD), QK_HEAD).astype(BF16)
    w_uk = w_ukv[layer][:, :, :QK_NOPE]
    w_uv = w_ukv[layer][:, :, QK_NOPE:]
    wuk_groups = _head_groups(w_uk, QK_NOPE).astype(BF16)
    wuv_t = w_uv.reshape(KV_LORA, N_HEADS * V_HEAD).T.astype(BF16)
    wuv_heads = w_uv.transpose(1, 0, 2).astype(BF16)
    wukt = jnp.pad(w_uk.transpose(1, 2, 0), ((0, 0), (0, HEAD_SLAB - QK_NOPE), (0, 0))).astype(BF16)
    qg = q_norm_g[layer][None, :]
    kvg = kv_norm_g[layer][None, :]
    convw = conv_w[layer]
    in_wts = (wa, wb, qg, kvg, wuq, convw)
    out_wts = (wg, w_oa[layer].astype(BF16), w_oc[layer].astype(BF16), w_o[layer].astype(BF16),
               ln1_g[layer][None, :], ln1_b[layer][None, :])
    ffn_wts = (w_ff1[layer].astype(BF16), w_ff3[layer].astype(BF16), w_ff2[layer].astype(BF16),
               ln2_g[layer][None, :], ln2_b[layer][None, :])

    ada = _ada_terms(jnp.concatenate([c_prompt, c_sample], axis=0), w_ada[layer].astype(BF16), b_ada[layer][None, :])
    ada_p = [a[:, None, :] for a in jnp.split(ada[:nb_p], 6, axis=-1)]
    ada_s = [jnp.repeat(a, seq_s, axis=0) for a in jnp.split(ada[nb_p:], 6, axis=-1)]

    tm_p = 1024
    tps = seq_p // tm_p
    xp = x_prompt.reshape(nb_p * seq_p, D_MODEL)
    tabs_p = _rope_tables(jnp.arange(seq_p), ATTN_SCALE * math.log2(math.e))
    ckv_p, kr_p, cb_p, u_p, q_p, k_p, vt_p = _mixer_in(
        xp, ada_p[0], ada_p[1], in_wts, tabs_p, prompt=True, nb=nb_p, seq=seq_p, tm=1024, sub=256,
        extra=(wuk_groups, wuv_t))
    o_p = _prompt_attention(q_p, k_p, vt_p, nb=nb_p, seq=seq_p)
    x1_p = _mixer_out(xp, ada_p[0], ada_p[1], ada_p[2], o_p.reshape(nb_p * seq_p, N_HEADS * V_HEAD), cb_p, out_wts,
                      prompt=True, tm=tm_p, sub=256, tiles_per_seq=tps)
    y_p = _ffn(x1_p, ada_p[3], ada_p[4], ada_p[5], ffn_wts, prompt=True, tm=tm_p, sub=256, tiles_per_seq=tps)

    tm_s = 256
    ntok_s = nb_s * seq_s
    xs = x_sample.reshape(ntok_s, D_MODEL)
    tabs_s = tuple(jnp.tile(t, (tm_s // seq_s, 1)) for t in _rope_tables(past_len + jnp.arange(seq_s), ATTN_SCALE))
    st = state_conv[layer]
    zrow = jnp.zeros((nb_s, seq_s - 1, CONV_DIM), F32)
    fix1 = jnp.concatenate([st[:, 1:2], zrow], axis=1).reshape(ntok_s, CONV_DIM)
    fix2 = jnp.concatenate([st, zrow[:, 1:]], axis=1).reshape(ntok_s, CONV_DIM)
    ckv_s, kr_s, cb_s, u_s, qabs_s, qr_s = _mixer_in(
        xs, ada_s[0], ada_s[1], in_wts, tabs_s, prompt=False, nb=nb_s, seq=seq_s, tm=tm_s, sub=tm_s,
        extra=(wukt, fix1, fix2))
    rows = N_HEADS * seq_s
    olat = _decode_attention(
        page_table, qabs_s.reshape(nb_s, rows, KV_LORA), qr_s.reshape(nb_s, rows, QK_ROPE),
        ckv_s.reshape(nb_s, seq_s, KV_LORA), kr_s.reshape(nb_s, seq_s, QK_ROPE),
        cache_kv_latent.reshape(n_pool, page, KV_LORA),
        jnp.swapaxes(cache_k_rope.reshape(n_pool, page, QK_ROPE), 1, 2))
    o_s = _uv_project(olat.reshape(nb_s, N_HEADS, seq_s, KV_LORA), wuv_heads)
    x1_s = _mixer_out(xs, ada_s[0], ada_s[1], ada_s[2], o_s, cb_s, out_wts, prompt=False, tm=tm_s, sub=tm_s, tiles_per_seq=1)
    y_s = _ffn(x1_s, ada_s[3], ada_s[4], ada_s[5], ffn_wts, prompt=False, tm=tm_s, sub=tm_s, tiles_per_seq=1)

    hist = CONV_WIDTH - 1
    return (y_p.reshape(nb_p, seq_p, D_MODEL),
            y_s.reshape(nb_s, seq_s, D_MODEL),
            ckv_p.reshape(1, nb_p, seq_p, KV_LORA),
            kr_p.reshape(1, nb_p, seq_p, QK_ROPE),
            u_p.reshape(1, nb_p, SUBLANES, CONV_DIM)[:, :, SUBLANES - hist:],
            ckv_s.reshape(1, nb_s, seq_s, KV_LORA),
            kr_s.reshape(1, nb_s, seq_s, QK_ROPE),
            u_s.reshape(1, nb_s, seq_s, CONV_DIM)[:, :, seq_s - hist:])
```

```python
import functools
import math

import jax
import jax.numpy as jnp
from jax import lax
from jax.experimental import pallas as pl
from jax.experimental.pallas import tpu as pltpu

D_MODEL = 1024
N_HEADS = 8
QK_NOPE = 64
QK_ROPE = 32
QK_HEAD = QK_NOPE + QK_ROPE
V_HEAD = 64
Q_LORA = 384
KV_LORA = 256
ROPE_THETA = 10000.0
ATTN_SCALE = QK_HEAD ** -0.5
CONV_DIM = 512
CONV_WIDTH = 3
D_FF = int(math.ceil(8 * D_MODEL / 3 / 256)) * 256
DEPTH = 1
ALPHA = (2.0 * DEPTH) ** 0.25
LN_EPS = 1e-5
RMS_EPS = 1e-6

LANES = 128
SUBLANES = 8
HEAD_SLAB = LANES
ROPE_LO = QK_NOPE
ROPE_HALF = QK_ROPE // 2
VMEM_LIMIT = 52 * 1024 * 1024

PROMPT_TILE = 1024
SAMPLE_TILE = 256
SUB_TILE = 256
FFN_SUB_TILE = 128
ATTN_TILE = 512
ATTN_HEADS = 4
ADA_COLS = 1536
DECODE_GROUP = 16
DECODE_SLOTS = 4
DECODE_LANES = 2

BF16 = jnp.bfloat16
F32 = jnp.float32
NT_DIMS = (((1,), (1,)), ((), ()))


def _params(n_axes=1, vmem=VMEM_LIMIT):
    return pltpu.CompilerParams(dimension_semantics=("arbitrary",) * n_axes, vmem_limit_bytes=vmem)


def _const_spec(shape):
    nd = len(shape)
    return pl.BlockSpec(shape, lambda *_: (0,) * nd, pipeline_mode=pl.Buffered(1))


def _mm(a, b):
    return jnp.dot(a, b, preferred_element_type=F32)


def _sigmoid(x):
    return 1.0 / (1.0 + jnp.exp(-x))


def _layer_norm(x, g, b):
    mu = jnp.mean(x, axis=-1, keepdims=True)
    xc = x - mu
    var = jnp.mean(xc * xc, axis=-1, keepdims=True)
    return xc * lax.rsqrt(var + LN_EPS) * g + b


def _rms_norm(x, g):
    return x * lax.rsqrt(jnp.mean(x * x, axis=-1, keepdims=True) + RMS_EPS) * g


def _rows(ref, r, tm):
    return ref[r, :] if ref.shape[0] == tm else ref[...]


def _pipelined(tm, sub, front, back):
    subs = [slice(r0, r0 + sub) for r0 in range(0, tm, sub)]
    pending = front(subs[0])
    for n, r in enumerate(subs):
        nxt = front(subs[n + 1]) if n + 1 < len(subs) else None
        back(r, *pending)
        pending = nxt


def _rope_slab(x, cos_t, sin_t):
    lane = lax.broadcasted_iota(jnp.int32, x.shape, 1)
    swapped = jnp.where(lane < ROPE_LO + ROPE_HALF,
                        pltpu.roll(x, LANES - ROPE_HALF, 1), pltpu.roll(x, ROPE_HALF, 1))
    return x * cos_t + swapped * sin_t


def _ada_kernel(c_ref, w_ref, b_ref, o_ref):
    c = c_ref[...]
    s = c * _sigmoid(c)
    o_ref[...] = _mm(s.astype(BF16), w_ref[...]) + b_ref[...]


def _ada_terms(c, w_ada, b_ada):
    n, tn = c.shape[0], ADA_COLS
    return pl.pallas_call(
        _ada_kernel,
        grid=(6 * D_MODEL // tn,),
        in_specs=[pl.BlockSpec((n, D_MODEL), lambda j: (0, 0)),
                  pl.BlockSpec((D_MODEL, tn), lambda j: (0, j)),
                  pl.BlockSpec((1, tn), lambda j: (0, j))],
        out_specs=pl.BlockSpec((n, tn), lambda j: (0, j)),
        out_shape=jax.ShapeDtypeStruct((n, 6 * D_MODEL), F32),
        compiler_params=_params(),
        name="ada_terms",
    )(c, w_ada, b_ada)


def _mixer_in_kernel(*refs, prompt, tm, sub, tiles_per_seq):
    if prompt:
        (x_ref, sh_ref, sc_ref, wa_ref, wb_ref, qg_ref, kvg_ref, wuq_ref, convw_ref,
         cq_ref, sq_ref, ck_ref, sk_ref, wuk_ref, wuvt_ref,
         ckv_ref, kr_ref, cb_ref, u_ref, q_ref, k_ref, vt_ref, ubuf) = refs
    else:
        (x_ref, sh_ref, sc_ref, wa_ref, wb_ref, qg_ref, kvg_ref, wuq_ref, convw_ref,
         cq_ref, sq_ref, ck_ref, sk_ref, wukt_ref, fix1_ref, fix2_ref,
         ckv_ref, kr_ref, cb_ref, u_ref, qabs_ref, qr_ref, ubuf) = refs
    i = pl.program_id(0)

    @pl.when(i % tiles_per_seq == 0)
    def _():
        ubuf[0:SUBLANES, :] = jnp.zeros((SUBLANES, CONV_DIM), F32)

    def front(r):
        h = (x_ref[r, :] * (1.0 + _rows(sc_ref, r, tm)) + _rows(sh_ref, r, tm)).astype(BF16)
        return _mm(h, wa_ref[...]), _mm(h, wb_ref[...])

    def back(r, za, zb):
        r0, n = r.start, r.stop - r.start
        q_lat = za[:, :Q_LORA]
        kv_lat = za[:, Q_LORA:Q_LORA + KV_LORA]
        kr_slab = za[:, Q_LORA + KV_LORA:]
        c_kv = _rms_norm(kv_lat, kvg_ref[...])
        ckv_ref[r, :] = c_kv
        kr_rot = _rope_slab(kr_slab, ck_ref[r, :], sk_ref[r, :])
        kr_ref[r, :] = kr_rot[:, ROPE_LO:ROPE_LO + QK_ROPE]

        q_all = _mm(_rms_norm(q_lat, qg_ref[...]).astype(BF16), wuq_ref[...])
        cq, sq = cq_ref[r, :], sq_ref[r, :]
        q_heads = [_rope_slab(q_all[:, hd * HEAD_SLAB:(hd + 1) * HEAD_SLAB], cq, sq) for hd in range(N_HEADS)]

        b_g = zb[:, :CONV_DIM]
        u = zb[:, CONV_DIM:2 * CONV_DIM] * zb[:, 2 * CONV_DIM:]
        ubuf[SUBLANES + r0:SUBLANES + r0 + n, :] = u
        u1 = ubuf[pl.ds(SUBLANES + r0 - 1, n), :]
        u2 = ubuf[pl.ds(SUBLANES + r0 - 2, n), :]
        if prompt:
            if r.stop == tm:
                ubuf[0:SUBLANES, :] = u[n - SUBLANES:, :]
                u_ref[...] = u[n - SUBLANES:, :]
        else:
            t = lax.broadcasted_iota(jnp.int32, (n, CONV_DIM), 0) % SUBLANES
            u1 = jnp.where(t == 0, fix1_ref[r, :], u1)
            u2 = jnp.where(t < 2, fix2_ref[r, :], u2)
            u_ref[r, :] = u
        cw = convw_ref[...]
        conv_y = u2 * cw[0:1, :] + u1 * cw[1:2, :] + u * cw[2:3, :]
        cb_ref[r, :] = (b_g * conv_y).astype(BF16)

        if prompt:
            c_bf = c_kv.astype(BF16)
            k_nope = _mm(c_bf, wuk_ref[...])
            vt_ref[:, r] = lax.dot_general(wuvt_ref[...], c_bf, NT_DIMS, preferred_element_type=F32).astype(BF16)
            for hd in range(N_HEADS):
                q_ref[hd, r, :] = q_heads[hd].astype(BF16)
                k_ref[hd, r, :] = (k_nope[:, hd * HEAD_SLAB:(hd + 1) * HEAD_SLAB] + kr_rot).astype(BF16)
        else:
            seqs = slice(r0 // SUBLANES, r.stop // SUBLANES)
            for hd in range(N_HEADS):
                qh = q_heads[hd]
                qabs = _mm(qh.astype(BF16), wukt_ref[hd])
                qabs_ref[seqs, hd] = qabs.reshape(n // SUBLANES, SUBLANES, KV_LORA)
                qr_ref[seqs, hd] = qh[:, ROPE_LO:ROPE_LO + QK_ROPE].reshape(n // SUBLANES, SUBLANES, QK_ROPE)

    _pipelined(tm, sub, front, back)


def _mixer_in(x2d, sh, sc, wts, tabs, *, prompt, nb, seq, tm, sub, extra):
    ntok = nb * seq
    tiles_per_seq = seq // tm if prompt else 1
    grid = (ntok // tm,)
    row = lambda w: pl.BlockSpec((tm, w), lambda i: (i, 0))
    if prompt:
        ada_spec = pl.BlockSpec((None, 1, D_MODEL), lambda i: (i // tiles_per_seq, 0, 0))
        tab_spec = pl.BlockSpec((tm, LANES), lambda i: (i % tiles_per_seq, 0))
    else:
        ada_spec = row(D_MODEL)
        tab_spec = pl.BlockSpec((tm, LANES), lambda i: (0, 0))
    wa, wb, qg, kvg, wuq, convw = wts
    in_specs = [row(D_MODEL), ada_spec, ada_spec,
                _const_spec(wa.shape), _const_spec(wb.shape), _const_spec(qg.shape), _const_spec(kvg.shape),
                _const_spec(wuq.shape), _const_spec(convw.shape),
                tab_spec, tab_spec, tab_spec, tab_spec]
    out_shape = [jax.ShapeDtypeStruct((ntok, KV_LORA), F32), jax.ShapeDtypeStruct((ntok, QK_ROPE), F32),
                 jax.ShapeDtypeStruct((ntok, CONV_DIM), BF16)]
    out_specs = [row(KV_LORA), row(QK_ROPE), row(CONV_DIM)]
    if prompt:
        wuk, wuv = extra
        in_specs += [_const_spec(wuk.shape), _const_spec(wuv.shape)]
        head_spec = pl.BlockSpec((None, N_HEADS, tm, HEAD_SLAB),
                                 lambda i: (i // tiles_per_seq, 0, i % tiles_per_seq, 0))
        out_shape += [jax.ShapeDtypeStruct((nb * SUBLANES, CONV_DIM), F32),
                      jax.ShapeDtypeStruct((nb, N_HEADS, seq, HEAD_SLAB), BF16),
                      jax.ShapeDtypeStruct((nb, N_HEADS, seq, HEAD_SLAB), BF16),
                      jax.ShapeDtypeStruct((nb, N_HEADS * V_HEAD, seq), BF16)]
        out_specs += [pl.BlockSpec((SUBLANES, CONV_DIM), lambda i: (i // tiles_per_seq, 0)),
                      head_spec, head_spec,
                      pl.BlockSpec((None, N_HEADS * V_HEAD, tm), lambda i: (i // tiles_per_seq, 0, i % tiles_per_seq))]
    else:
        wukt, fix1, fix2 = extra
        in_specs += [_const_spec(wukt.shape), row(CONV_DIM), row(CONV_DIM)]
        nseq = tm // SUBLANES
        out_shape += [jax.ShapeDtypeStruct((ntok, CONV_DIM), F32),
                      jax.ShapeDtypeStruct((nb, N_HEADS, SUBLANES, KV_LORA), F32),
                      jax.ShapeDtypeStruct((nb, N_HEADS, SUBLANES, QK_ROPE), F32)]
        out_specs += [row(CONV_DIM),
                      pl.BlockSpec((nseq, N_HEADS, SUBLANES, KV_LORA), lambda i: (i, 0, 0, 0)),
                      pl.BlockSpec((nseq, N_HEADS, SUBLANES, QK_ROPE), lambda i: (i, 0, 0, 0))]
    return pl.pallas_call(
        functools.partial(_mixer_in_kernel, prompt=prompt, tm=tm, sub=sub, tiles_per_seq=tiles_per_seq),
        grid=grid, in_specs=in_specs, out_specs=out_specs, out_shape=out_shape,
        scratch_shapes=[pltpu.VMEM((SUBLANES + tm, CONV_DIM), F32)],
        compiler_params=_params(),
        name="mixer_in_prompt" if prompt else "mixer_in_sample",
    )(x2d, sh, sc, wa, wb, qg, kvg, wuq, convw, *tabs, *extra)


def _prompt_attn_kernel(q_ref, k_ref, vt_ref, o_ref, *, seq, tq, heads):
    th = tq // 2
    key_le_query = (lax.broadcasted_iota(jnp.int32, (th, th), 0) <= lax.broadcasted_iota(jnp.int32, (th, th), 1))

    def scores(i, hh):
        lo, mid, hi = i * tq, i * tq + th, (i + 1) * tq
        q = q_ref[hh, lo:hi, :]
        s_aa = lax.dot_general(k_ref[hh, lo:mid, :], q[0:th], NT_DIMS, preferred_element_type=F32)
        s_b = lax.dot_general(k_ref[hh, lo:hi, :], q[th:tq], NT_DIMS, preferred_element_type=F32)
        s_aa = jnp.where(key_le_query, s_aa, -jnp.inf)
        s_bb = jnp.where(key_le_query, s_b[th:tq], -jnp.inf)
        so = lax.dot_general(k_ref[hh, 0:lo, :], q, NT_DIMS, preferred_element_type=F32) if i > 0 else None
        return s_aa, s_b[0:th], s_bb, so

    def softmax_pv(i, hh, s_aa, s_ab, s_bb, so):
        lo, mid, hi = i * tq, i * tq + th, (i + 1) * tq
        vt = vt_ref.at[hh * V_HEAD:(hh + 1) * V_HEAD, :]
        m_a = jnp.max(s_aa, axis=0, keepdims=True)
        m_b = jnp.maximum(jnp.max(s_ab, axis=0, keepdims=True), jnp.max(s_bb, axis=0, keepdims=True))
        m = jnp.concatenate([m_a, m_b], axis=1)
        if so is not None:
            m = jnp.maximum(m, jnp.max(so, axis=0, keepdims=True))
        m_a, m_b = m[:, 0:th], m[:, th:tq]
        p_aa = jnp.exp2(s_aa - m_a)
        p_ab = jnp.exp2(s_ab - m_b)
        p_bb = jnp.exp2(s_bb - m_b)
        l = jnp.concatenate([jnp.sum(p_aa, axis=0, keepdims=True),
                             jnp.sum(p_ab, axis=0, keepdims=True) + jnp.sum(p_bb, axis=0, keepdims=True)], axis=1)
        p_b = jnp.concatenate([p_ab, p_bb], axis=0).astype(BF16)
        acc = jnp.concatenate([_mm(vt[:, lo:mid], p_aa.astype(BF16)), _mm(vt[:, lo:hi], p_b)], axis=1)
        if so is not None:
            po = jnp.exp2(so - m)
            l = l + jnp.sum(po, axis=0, keepdims=True)
            acc = acc + _mm(vt[:, 0:lo], po.astype(BF16))
        return acc * (1.0 / l)

    n_tiles = seq // tq
    units = [(i, hh) for i in reversed(range(n_tiles)) for hh in range(heads)]
    pending = scores(*units[0])
    outs = []
    for n, (i, hh) in enumerate(units):
        nxt = scores(*units[n + 1]) if n + 1 < len(units) else None
        outs.append(softmax_pv(i, hh, *pending))
        pending = nxt
        if hh % 2 == 1:
            o_ref[i * tq:(i + 1) * tq, (hh // 2) * LANES:(hh // 2 + 1) * LANES] = (
                jnp.concatenate(outs, axis=0).T.astype(BF16))
            outs = []


def _prompt_attention(q, k, vt, *, nb, seq, tq=ATTN_TILE, heads=ATTN_HEADS):
    head_spec = pl.BlockSpec((None, heads, seq, HEAD_SLAB), lambda b, p: (b, p, 0, 0))
    return pl.pallas_call(
        functools.partial(_prompt_attn_kernel, seq=seq, tq=tq, heads=heads),
        grid=(nb, N_HEADS // heads),
        in_specs=[head_spec, head_spec, pl.BlockSpec((None, heads * V_HEAD, seq), lambda b, p: (b, p, 0))],
        out_specs=pl.BlockSpec((None, seq, heads * V_HEAD), lambda b, p: (b, 0, p)),
        out_shape=jax.ShapeDtypeStruct((nb, seq, N_HEADS * V_HEAD), BF16),
        compiler_params=_params(2),
        name="prompt_attention",
    )(q, k, vt)


def _decode_attn_kernel(pt_ref, qabs_ref, qr_ref, cnew_ref, krnew_ref, lat_hbm, ropet_hbm, olat_ref, *bufs,
                        n_pages, group, page, n_slots, n_steps, lanes):
    b = pl.program_id(0)
    n_groups = n_pages // group
    gk = group * page
    rows = N_HEADS * SUBLANES
    per_lane = 6 + 2 * n_groups
    lane_bufs = [bufs[j * per_lane:(j + 1) * per_lane] for j in range(lanes)]

    def page_copies(j, step, g):
        latbuf, ropebuf, sems = lane_bufs[j][:3]
        slot = g % n_slots
        out = []
        for p in range(group):
            pid = pt_ref[step * lanes + j, g * group + p]
            out.append(pltpu.make_async_copy(lat_hbm.at[pid], latbuf.at[slot, pl.ds(p * page, page)], sems.at[0, slot]))
            out.append(pltpu.make_async_copy(ropet_hbm.at[pid], ropebuf.at[slot, :, pl.ds(p * page, page)], sems.at[1, slot]))
        return out

    @pl.when(b == 0)
    def _():
        for g in range(n_slots):
            for j in range(lanes):
                for cp in page_copies(j, 0, g):
                    cp.start()

    def new_token_block(ref, j, width):
        return jnp.concatenate([ref[j], jnp.zeros((SUBLANES, width), F32)], axis=0).astype(BF16)

    def arrive(j, g):
        for cp in page_copies(j, b, g):
            cp.wait()

    def stage(j, g):
        latbuf, ropebuf = lane_bufs[j][:2]
        cbf, krbf = lane_bufs[j][5:5 + n_groups], lane_bufs[j][5 + n_groups:5 + 2 * n_groups]
        cbf[g][...] = latbuf[g % n_slots].astype(BF16)
        krbf[g][...] = ropebuf[g % n_slots].astype(BF16)

    def refill(j, g):
        nxt = g + n_slots
        if nxt < n_groups:
            for cp in page_copies(j, b, nxt):
                cp.start()
        else:
            @pl.when(b + 1 < n_steps)
            def _():
                for cp in page_copies(j, b + 1, nxt - n_groups):
                    cp.start()

    def phase1():
        qa = [qabs_ref[j].astype(BF16) for j in range(lanes)]
        qr = [qr_ref[j].astype(BF16) for j in range(lanes)]
        t_row = lax.broadcasted_iota(jnp.int32, (rows, 2 * SUBLANES), 0) % SUBLANES
        k_col = lax.broadcasted_iota(jnp.int32, (rows, 2 * SUBLANES), 1)
        for j in range(lanes):
            s_new = (lax.dot_general(qa[j], new_token_block(cnew_ref, j, KV_LORA), NT_DIMS, preferred_element_type=F32)
                     + lax.dot_general(qr[j], new_token_block(krnew_ref, j, QK_ROPE), NT_DIMS,
                                       preferred_element_type=F32))
            lane_bufs[j][4][...] = jnp.where(k_col <= t_row, s_new, -jnp.inf)
            arrive(j, 0)
        for j in range(lanes):
            stage(j, 0)
        m_lanes = [None] * lanes
        for g in range(n_groups):
            for j in range(lanes):
                refill(j, g)
            if g + 1 < n_groups:
                for j in range(lanes):
                    arrive(j, g + 1)
            for j in range(lanes):
                s_all = lane_bufs[j][3]
                cbf, krbf = lane_bufs[j][5:5 + n_groups], lane_bufs[j][5 + n_groups:5 + 2 * n_groups]
                s = lax.dot_general(qa[j], cbf[g][...], NT_DIMS, preferred_element_type=F32) + _mm(qr[j], krbf[g][...])
                s_all[:, g * gk:(g + 1) * gk] = s
                for k in range(gk // LANES):
                    blk = s[:, k * LANES:(k + 1) * LANES]
                    m_lanes[j] = blk if m_lanes[j] is None else jnp.maximum(m_lanes[j], blk)
                if g + 1 < n_groups:
                    stage(j, g + 1)
        for j in range(lanes):
            lane_bufs[j][5 + 2 * n_groups][...] = m_lanes[j]

    def phase2():
        m, l_new, l_lanes, acc = [], [], [None] * lanes, []
        for j in range(lanes):
            sn = lane_bufs[j][4][...]
            m_part = lane_bufs[j][5 + 2 * n_groups][...]
            m.append(jnp.maximum(jnp.max(sn, axis=-1, keepdims=True), jnp.max(m_part, axis=-1, keepdims=True)))
            p = jnp.exp(sn - m[j])
            l_new.append(jnp.sum(p, axis=-1, keepdims=True))
            acc.append(_mm(p.astype(BF16), new_token_block(cnew_ref, j, KV_LORA)))
        units = [(g, j) for g in range(n_groups) for j in range(lanes)]
        pending = None
        for n in range(len(units) + 1):
            if n < len(units):
                g, j = units[n]
                p = jnp.exp(lane_bufs[j][3][:, g * gk:(g + 1) * gk] - m[j])
                for k in range(gk // LANES):
                    blk = p[:, k * LANES:(k + 1) * LANES]
                    l_lanes[j] = blk if l_lanes[j] is None else l_lanes[j] + blk
                nxt = (g, j, p.astype(BF16))
            else:
                nxt = None
            if pending is not None:
                g0, j0, p0 = pending
                acc[j0] = acc[j0] + _mm(p0, lane_bufs[j0][5 + g0][...])
            pending = nxt
        for j in range(lanes):
            l = l_new[j] + jnp.sum(l_lanes[j], axis=-1, keepdims=True)
            olat_ref[j] = acc[j] * (1.0 / l)

    pl.when(b >= 0)(phase1)
    pl.when(b < n_steps)(phase2)


def _decode_attention(page_table, qabs, qr, c_new, kr_new, cache_lat, cache_rope_t, *,
                      group=DECODE_GROUP, n_slots=DECODE_SLOTS, lanes=DECODE_LANES):
    nb, n_pages = page_table.shape
    page = cache_lat.shape[1]
    rows = N_HEADS * SUBLANES
    n_groups = n_pages // group
    assert n_pages % group == 0 and n_groups % n_slots == 0 and nb % lanes == 0
    per_step = lambda w, r: pl.BlockSpec((lanes, r, w), lambda b, pt: (b, 0, 0))
    lane_scratch = ([pltpu.VMEM((n_slots, group * page, KV_LORA), F32),
                     pltpu.VMEM((n_slots, QK_ROPE, group * page), F32),
                     pltpu.SemaphoreType.DMA((2, n_slots)),
                     pltpu.VMEM((rows, n_pages * page), F32),
                     pltpu.VMEM((rows, 2 * SUBLANES), F32)]
                    + [pltpu.VMEM((group * page, KV_LORA), BF16)] * n_groups
                    + [pltpu.VMEM((QK_ROPE, group * page), BF16)] * n_groups
                    + [pltpu.VMEM((rows, LANES), F32)])
    grid_spec = pltpu.PrefetchScalarGridSpec(
        num_scalar_prefetch=1,
        grid=(nb // lanes,),
        in_specs=[per_step(KV_LORA, rows), per_step(QK_ROPE, rows), per_step(KV_LORA, SUBLANES),
                  per_step(QK_ROPE, SUBLANES),
                  pl.BlockSpec(memory_space=pl.ANY), pl.BlockSpec(memory_space=pl.ANY)],
        out_specs=per_step(KV_LORA, rows),
        scratch_shapes=lane_scratch * lanes,
    )
    return pl.pallas_call(
        functools.partial(_decode_attn_kernel, n_pages=n_pages, group=group, page=page, n_slots=n_slots,
                          n_steps=nb // lanes, lanes=lanes),
        grid_spec=grid_spec,
        out_shape=jax.ShapeDtypeStruct((nb, rows, KV_LORA), F32),
        compiler_params=_params(),
        name="decode_attention",
    )(page_table, qabs, qr, c_new, kr_new, cache_lat, cache_rope_t)


def _uv_kernel(olat_ref, wuv_ref, o_ref):
    n = olat_ref.shape[0] * SUBLANES
    for hd in range(N_HEADS):
        x = olat_ref[:, hd].reshape(n, KV_LORA).astype(BF16)
        o_ref[:, hd * V_HEAD:(hd + 1) * V_HEAD] = _mm(x, wuv_ref[hd]).astype(BF16)


def _uv_project(olat, wuv_heads):
    nb = olat.shape[0]
    return pl.pallas_call(
        _uv_kernel,
        grid=(1,),
        in_specs=[pl.BlockSpec(olat.shape, lambda i: (0, 0, 0, 0)),
                  pl.BlockSpec(wuv_heads.shape, lambda i: (0, 0, 0))],
        out_specs=pl.BlockSpec((nb * SUBLANES, N_HEADS * V_HEAD), lambda i: (0, 0)),
        out_shape=jax.ShapeDtypeStruct((nb * SUBLANES, N_HEADS * V_HEAD), BF16),
        compiler_params=_params(),
        name="uv_project",
    )(olat, wuv_heads)


def _mixer_out_kernel(x_ref, sh_ref, sc_ref, g1_ref, o_ref, cb_ref, wg_ref, woa_ref, woc_ref, wo_ref,
                      lng_ref, lnb_ref, x1_ref, *, tm, sub):
    def front(r):
        h = (x_ref[r, :] * (1.0 + _rows(sc_ref, r, tm)) + _rows(sh_ref, r, tm)).astype(BF16)
        return _mm(h, wg_ref[...]), _mm(o_ref[r, :], woa_ref[...]), _mm(cb_ref[r, :], woc_ref[...])

    def back(r, gates, y_a, y_c):
        mixed = _sigmoid(gates[:, :D_MODEL]) * y_a + _sigmoid(gates[:, D_MODEL:]) * y_c
        mix = _mm(mixed.astype(BF16), wo_ref[...])
        x1_ref[r, :] = _layer_norm(ALPHA * x_ref[r, :] + _rows(g1_ref, r, tm) * mix, lng_ref[...], lnb_ref[...])

    _pipelined(tm, sub, front, back)


def _mixer_out(x2d, sh, sc, g1, o, cb, wts, *, prompt, tm, sub, tiles_per_seq):
    ntok = x2d.shape[0]
    row = lambda w: pl.BlockSpec((tm, w), lambda i: (i, 0))
    ada_spec = (pl.BlockSpec((None, 1, D_MODEL), lambda i: (i // tiles_per_seq, 0, 0)) if prompt else row(D_MODEL))
    return pl.pallas_call(
        functools.partial(_mixer_out_kernel, tm=tm, sub=sub),
        grid=(ntok // tm,),
        in_specs=[row(D_MODEL), ada_spec, ada_spec, ada_spec, row(N_HEADS * V_HEAD), row(CONV_DIM)]
                 + [_const_spec(w.shape) for w in wts],
        out_specs=row(D_MODEL),
        out_shape=jax.ShapeDtypeStruct((ntok, D_MODEL), F32),
        compiler_params=_params(),
        name="mixer_out_prompt" if prompt else "mixer_out_sample",
    )(x2d, sh, sc, g1, o, cb, *wts)


def _ffn_kernel(x1_ref, sh_ref, sc_ref, g2_ref, w1_ref, w3_ref, w2_ref, lng_ref, lnb_ref, y_ref, *, tm, sub):
    def front(r):
        h2 = (x1_ref[r, :] * (1.0 + _rows(sc_ref, r, tm)) + _rows(sh_ref, r, tm)).astype(BF16)
        return _mm(h2, w1_ref[...]), _mm(h2, w3_ref[...])

    def back(r, a, g):
        gated = (a * _sigmoid(a)) * g
        f = _mm(gated.astype(BF16), w2_ref[...])
        y_ref[r, :] = _layer_norm(ALPHA * x1_ref[r, :] + _rows(g2_ref, r, tm) * f, lng_ref[...], lnb_ref[...])

    _pipelined(tm, sub, front, back)


def _ffn(x1, sh, sc, g2, wts, *, prompt, tm, sub, tiles_per_seq):
    ntok = x1.shape[0]
    row = lambda w: pl.BlockSpec((tm, w), lambda i: (i, 0))
    ada_spec = (pl.BlockSpec((None, 1, D_MODEL), lambda i: (i // tiles_per_seq, 0, 0)) if prompt else row(D_MODEL))
    return pl.pallas_call(
        functools.partial(_ffn_kernel, tm=tm, sub=sub),
        grid=(ntok // tm,),
        in_specs=[row(D_MODEL), ada_spec, ada_spec, ada_spec] + [_const_spec(w.shape) for w in wts],
        out_specs=row(D_MODEL),
        out_shape=jax.ShapeDtypeStruct((ntok, D_MODEL), F32),
        compiler_params=_params(),
        name="ffn_prompt" if prompt else "ffn_sample",
    )(x1, sh, sc, g2, *wts)


def _rope_tables(pos, scale_q):
    inv = ROPE_THETA ** (-jnp.arange(0, QK_ROPE, 2, dtype=F32) / QK_ROPE)
    ang = pos.astype(F32)[:, None] * inv[None, :]
    cos, sin = jnp.cos(ang), jnp.sin(ang)
    n = pos.shape[0]
    z_lo = jnp.zeros((n, ROPE_LO), F32)
    z_hi = jnp.zeros((n, LANES - ROPE_LO - QK_ROPE), F32)
    cos_k = jnp.concatenate([z_lo, cos, cos, z_hi], axis=1)
    sin_k = jnp.concatenate([z_lo, -sin, sin, z_hi], axis=1)
    cos_q = jnp.concatenate([jnp.ones((n, ROPE_LO), F32), cos, cos, z_hi], axis=1) * scale_q
    sin_q = sin_k * scale_q
    return cos_q, sin_q, cos_k, sin_k


def _head_groups(w, width):
    k = w.shape[0]
    return jnp.pad(w, ((0, 0), (0, 0), (0, HEAD_SLAB - width))).reshape(k, N_HEADS * HEAD_SLAB)


def kernel(x_prompt, x_sample, cache_kv_latent, cache_k_rope, state_conv, page_table, c_prompt, c_sample,
           w_ada, b_ada, w_in, q_norm_g, kv_norm_g, w_uq, w_ukv, w_oa, conv_w, w_oc, w_o,
           ln1_g, ln1_b, w_ff1, w_ff3, w_ff2, ln2_g, ln2_b):
    assert w_ada.shape[0] == DEPTH == 1
    nb_p, seq_p, _ = x_prompt.shape
    nb_s, seq_s, _ = x_sample.shape
    assert seq_s == SUBLANES
    n_pool, page = cache_kv_latent.shape[1], cache_kv_latent.shape[2]
    past_len = page_table.shape[1] * page
    layer = 0

    s0, s1, s2, s3 = Q_LORA, Q_LORA + KV_LORA, Q_LORA + KV_LORA + QK_ROPE, Q_LORA + KV_LORA + QK_ROPE + 3 * CONV_DIM
    w_in_l = w_in[layer]
    w_kr_group = jnp.pad(w_in_l[:, s1:s2], ((0, 0), (ROPE_LO, LANES - ROPE_LO - QK_ROPE)))
    wa = jnp.concatenate([w_in_l[:, :s1], w_kr_group], axis=1).astype(BF16)
    wb = w_in_l[:, s2:s3].astype(BF16)
    wg = w_in_l[:, s3:].astype(BF16)
    wuq = _head_groups(w_uq[layer].reshape(Q_LORA, N_HEADS, QK_HEAD), QK_HEAD).astype(BF16)
    w_uk = w_ukv[layer][:, :, :QK_NOPE]
    w_uv = w_ukv[layer][:, :, QK_NOPE:]
    wuk_groups = _head_groups(w_uk, QK_NOPE).astype(BF16)
    wuv_t = w_uv.reshape(KV_LORA, N_HEADS * V_HEAD).T.astype(BF16)
    wuv_heads = w_uv.transpose(1, 0, 2).astype(BF16)
    wukt = jnp.pad(w_uk.transpose(1, 2, 0), ((0, 0), (0, HEAD_SLAB - QK_NOPE), (0, 0))).astype(BF16)
    qg = q_norm_g[layer][None, :]
    kvg = kv_norm_g[layer][None, :]
    convw = conv_w[layer]
    in_wts = (wa, wb, qg, kvg, wuq, convw)
    out_wts = (wg, w_oa[layer].astype(BF16), w_oc[layer].astype(BF16), w_o[layer].astype(BF16),
               ln1_g[layer][None, :], ln1_b[layer][None, :])
    ffn_wts = (w_ff1[layer].astype(BF16), w_ff3[layer].astype(BF16), w_ff2[layer].astype(BF16),
               ln2_g[layer][None, :], ln2_b[layer][None, :])

    ada = _ada_terms(jnp.concatenate([c_prompt, c_sample], axis=0), w_ada[layer].astype(BF16), b_ada[layer][None, :])
    ada_p = [a[:, None, :] for a in jnp.split(ada[:nb_p], 6, axis=-1)]
    ada_s = [jnp.repeat(a, seq_s, axis=0) for a in jnp.split(ada[nb_p:], 6, axis=-1)]

    tm_p = PROMPT_TILE
    assert seq_p % tm_p == 0 and tm_p % SUB_TILE == 0 and seq_p % ATTN_TILE == 0
    tps = seq_p // tm_p
    xp = x_prompt.reshape(nb_p * seq_p, D_MODEL)
    tabs_p = _rope_tables(jnp.arange(seq_p), ATTN_SCALE * math.log2(math.e))
    ckv_p, kr_p, cb_p, u_p, q_p, k_p, vt_p = _mixer_in(
        xp, ada_p[0], ada_p[1], in_wts, tabs_p, prompt=True, nb=nb_p, seq=seq_p, tm=tm_p, sub=SUB_TILE,
        extra=(wuk_groups, wuv_t))
    o_p = _prompt_attention(q_p, k_p, vt_p, nb=nb_p, seq=seq_p)
    x1_p = _mixer_out(xp, ada_p[0], ada_p[1], ada_p[2], o_p.reshape(nb_p * seq_p, N_HEADS * V_HEAD), cb_p, out_wts,
                      prompt=True, tm=tm_p, sub=SUB_TILE, tiles_per_seq=tps)
    y_p = _ffn(x1_p, ada_p[3], ada_p[4], ada_p[5], ffn_wts, prompt=True, tm=tm_p, sub=FFN_SUB_TILE, tiles_per_seq=tps)

    tm_s = SAMPLE_TILE
    ntok_s = nb_s * seq_s
    assert ntok_s % tm_s == 0 and tm_s % seq_s == 0
    xs = x_sample.reshape(ntok_s, D_MODEL)
    tabs_s = tuple(jnp.tile(t, (tm_s // seq_s, 1)) for t in _rope_tables(past_len + jnp.arange(seq_s), ATTN_SCALE))
    st = state_conv[layer]
    zrow = jnp.zeros((nb_s, seq_s - 1, CONV_DIM), F32)
    fix1 = jnp.concatenate([st[:, 1:2], zrow], axis=1).reshape(ntok_s, CONV_DIM)
    fix2 = jnp.concatenate([st, zrow[:, 1:]], axis=1).reshape(ntok_s, CONV_DIM)
    ckv_s, kr_s, cb_s, u_s, qabs_s, qr_s = _mixer_in(
        xs, ada_s[0], ada_s[1], in_wts, tabs_s, prompt=False, nb=nb_s, seq=seq_s, tm=tm_s, sub=tm_s,
        extra=(wukt, fix1, fix2))
    rows = N_HEADS * seq_s
    olat = _decode_attention(
        page_table, qabs_s.reshape(nb_s, rows, KV_LORA), qr_s.reshape(nb_s, rows, QK_ROPE),
        ckv_s.reshape(nb_s, seq_s, KV_LORA), kr_s.reshape(nb_s, seq_s, QK_ROPE),
        cache_kv_latent.reshape(n_pool, page, KV_LORA),
        jnp.swapaxes(cache_k_rope.reshape(n_pool, page, QK_ROPE), 1, 2))
    o_s = _uv_project(olat.reshape(nb_s, N_HEADS, seq_s, KV_LORA), wuv_heads)
    x1_s = _mixer_out(xs, ada_s[0], ada_s[1], ada_s[2], o_s, cb_s, out_wts, prompt=False, tm=tm_s, sub=tm_s, tiles_per_seq=1)
    y_s = _ffn(x1_s, ada_s[3], ada_s[4], ada_s[5], ffn_wts, prompt=False, tm=tm_s, sub=tm_s, tiles_per_seq=1)

    hist = CONV_WIDTH - 1
    return (y_p.reshape(nb_p, seq_p, D_MODEL),
            y_s.reshape(nb_s, seq_s, D_MODEL),
            ckv_p.reshape(1, nb_p, seq_p, KV_LORA),
            kr_p.reshape(1, nb_p, seq_p, QK_ROPE),
            u_p.reshape(1, nb_p, SUBLANES, CONV_DIM)[:, :, SUBLANES - hist:],
            ckv_s.reshape(1, nb_s, seq_s, KV_LORA),
            kr_s.reshape(1, nb_s, seq_s, QK_ROPE),
            u_s.reshape(1, nb_s, seq_s, CONV_DIM)[:, :, seq_s - hist:])
```

```python
import functools
import math

import jax
import jax.numpy as jnp
from jax import lax
from jax.experimental import pallas as pl
from jax.experimental.pallas import tpu as pltpu

D_MODEL = 1024
N_HEADS = 8
QK_NOPE = 64
QK_ROPE = 32
QK_HEAD = QK_NOPE + QK_ROPE
V_HEAD = 64
Q_LORA = 384
KV_LORA = 256
ROPE_THETA = 10000.0
ATTN_SCALE = QK_HEAD ** -0.5
CONV_DIM = 512
CONV_WIDTH = 3
D_FF = int(math.ceil(8 * D_MODEL / 3 / 256)) * 256
DEPTH = 1
ALPHA = (2.0 * DEPTH) ** 0.25
LN_EPS = 1e-5
RMS_EPS = 1e-6

LANES = 128
SUBLANES = 8
HEAD_SLAB = LANES
ROPE_LO = QK_NOPE
ROPE_HALF = QK_ROPE // 2
VMEM_LIMIT = 52 * 1024 * 1024

PROMPT_TILE = 1024
SAMPLE_TILE = 256
SUB_TILE = 256
FFN_SUB_TILE = 128
ATTN_TILE = 512
ATTN_HEADS = 4
ADA_COLS = 1536
DECODE_GROUP = 16
DECODE_SLOTS = 4
DECODE_LANES = 2

BF16 = jnp.bfloat16
F32 = jnp.float32
NT_DIMS = (((1,), (1,)), ((), ()))


def _params(n_axes=1, vmem=VMEM_LIMIT):
    return pltpu.CompilerParams(dimension_semantics=("arbitrary",) * n_axes, vmem_limit_bytes=vmem)


def _const_spec(shape):
    nd = len(shape)
    return pl.BlockSpec(shape, lambda *_: (0,) * nd, pipeline_mode=pl.Buffered(1))


def _mm(a, b):
    return jnp.dot(a, b, preferred_element_type=F32)


def _sigmoid(x):
    return 1.0 / (1.0 + jnp.exp(-x))


def _layer_norm(x, g, b):
    mu = jnp.mean(x, axis=-1, keepdims=True)
    xc = x - mu
    var = jnp.mean(xc * xc, axis=-1, keepdims=True)
    return xc * lax.rsqrt(var + LN_EPS) * g + b


def _rms_norm(x, g):
    return x * lax.rsqrt(jnp.mean(x * x, axis=-1, keepdims=True) + RMS_EPS) * g


def _rows(ref, r, tm):
    return ref[r, :] if ref.shape[0] == tm else ref[...]


def _pipelined(tm, sub, front, back):
    subs = [slice(r0, r0 + sub) for r0 in range(0, tm, sub)]
    pending = front(subs[0])
    for n, r in enumerate(subs):
        nxt = front(subs[n + 1]) if n + 1 < len(subs) else None
        back(r, *pending)
        pending = nxt


def _rope_slab(x, cos_t, sin_t):
    lane = lax.broadcasted_iota(jnp.int32, x.shape, 1)
    swapped = jnp.where(lane < ROPE_LO + ROPE_HALF,
                        pltpu.roll(x, LANES - ROPE_HALF, 1), pltpu.roll(x, ROPE_HALF, 1))
    return x * cos_t + swapped * sin_t


def _ada_kernel(c_ref, w_ref, b_ref, o_ref):
    c = c_ref[...]
    s = c * _sigmoid(c)
    o_ref[...] = _mm(s.astype(BF16), w_ref[...]) + b_ref[...]


def _ada_terms(c, w_ada, b_ada):
    n, tn = c.shape[0], ADA_COLS
    return pl.pallas_call(
        _ada_kernel,
        grid=(6 * D_MODEL // tn,),
        in_specs=[pl.BlockSpec((n, D_MODEL), lambda j: (0, 0)),
                  pl.BlockSpec((D_MODEL, tn), lambda j: (0, j)),
                  pl.BlockSpec((1, tn), lambda j: (0, j))],
        out_specs=pl.BlockSpec((n, tn), lambda j: (0, j)),
        out_shape=jax.ShapeDtypeStruct((n, 6 * D_MODEL), F32),
        compiler_params=_params(),
        name="ada_terms",
    )(c, w_ada, b_ada)


def _mixer_in_kernel(*refs, prompt, tm, sub, tiles_per_seq):
    if prompt:
        (x_ref, sh_ref, sc_ref, wa_ref, wb_ref, qg_ref, kvg_ref, wuq_ref, convw_ref,
         cq_ref, sq_ref, ck_ref, sk_ref, wuk_ref, wuvt_ref,
         ckv_ref, kr_ref, cb_ref, u_ref, q_ref, k_ref, vt_ref, ubuf) = refs
    else:
        (x_ref, sh_ref, sc_ref, wa_ref, wb_ref, qg_ref, kvg_ref, wuq_ref, convw_ref,
         cq_ref, sq_ref, ck_ref, sk_ref, wukt_ref, fix1_ref, fix2_ref,
         ckv_ref, kr_ref, cb_ref, u_ref, qabs_ref, qr_ref, ubuf) = refs
    i = pl.program_id(0)

    @pl.when(i % tiles_per_seq == 0)
    def _():
        ubuf[0:SUBLANES, :] = jnp.zeros((SUBLANES, CONV_DIM), F32)

    def front(r):
        h = (x_ref[r, :] * (1.0 + _rows(sc_ref, r, tm)) + _rows(sh_ref, r, tm)).astype(BF16)
        return _mm(h, wa_ref[...]), _mm(h, wb_ref[...])

    def back(r, za, zb):
        r0, n = r.start, r.stop - r.start
        q_lat = za[:, :Q_LORA]
        kv_lat = za[:, Q_LORA:Q_LORA + KV_LORA]
        kr_slab = za[:, Q_LORA + KV_LORA:]
        c_kv = _rms_norm(kv_lat, kvg_ref[...])
        ckv_ref[r, :] = c_kv
        kr_rot = _rope_slab(kr_slab, ck_ref[r, :], sk_ref[r, :])
        kr_ref[r, :] = kr_rot[:, ROPE_LO:ROPE_LO + QK_ROPE]

        q_all = _mm(_rms_norm(q_lat, qg_ref[...]).astype(BF16), wuq_ref[...])
        cq, sq = cq_ref[r, :], sq_ref[r, :]
        q_heads = [_rope_slab(q_all[:, hd * HEAD_SLAB:(hd + 1) * HEAD_SLAB], cq, sq) for hd in range(N_HEADS)]

        b_g = zb[:, :CONV_DIM]
        u = zb[:, CONV_DIM:2 * CONV_DIM] * zb[:, 2 * CONV_DIM:]
        ubuf[SUBLANES + r0:SUBLANES + r0 + n, :] = u
        u1 = ubuf[pl.ds(SUBLANES + r0 - 1, n), :]
        u2 = ubuf[pl.ds(SUBLANES + r0 - 2, n), :]
        if prompt:
            if r.stop == tm:
                ubuf[0:SUBLANES, :] = u[n - SUBLANES:, :]
                u_ref[...] = u[n - SUBLANES:, :]
        else:
            t = lax.broadcasted_iota(jnp.int32, (n, CONV_DIM), 0) % SUBLANES
            u1 = jnp.where(t == 0, fix1_ref[r, :], u1)
            u2 = jnp.where(t < 2, fix2_ref[r, :], u2)
            u_ref[r, :] = u
        cw = convw_ref[...]
        conv_y = u2 * cw[0:1, :] + u1 * cw[1:2, :] + u * cw[2:3, :]
        cb_ref[r, :] = (b_g * conv_y).astype(BF16)

        if prompt:
            c_bf = c_kv.astype(BF16)
            k_nope = _mm(c_bf, wuk_ref[...])
            vt_ref[:, r] = lax.dot_general(wuvt_ref[...], c_bf, NT_DIMS, preferred_element_type=F32).astype(BF16)
            for hd in range(N_HEADS):
                q_ref[hd, r, :] = q_heads[hd].astype(BF16)
                k_ref[hd, r, :] = (k_nope[:, hd * HEAD_SLAB:(hd + 1) * HEAD_SLAB] + kr_rot).astype(BF16)
        else:
            seqs = slice(r0 // SUBLANES, r.stop // SUBLANES)
            for hd in range(N_HEADS):
                qh = q_heads[hd]
                qabs = _mm(qh.astype(BF16), wukt_ref[hd])
                qabs_ref[seqs, hd] = qabs.reshape(n // SUBLANES, SUBLANES, KV_LORA)
                qr_ref[seqs, hd] = qh[:, ROPE_LO:ROPE_LO + QK_ROPE].reshape(n // SUBLANES, SUBLANES, QK_ROPE)

    _pipelined(tm, sub, front, back)


def _mixer_in(x2d, sh, sc, wts, tabs, *, prompt, nb, seq, tm, sub, extra):
    ntok = nb * seq
    tiles_per_seq = seq // tm if prompt else 1
    grid = (ntok // tm,)
    row = lambda w: pl.BlockSpec((tm, w), lambda i: (i, 0))
    if prompt:
        ada_spec = pl.BlockSpec((None, 1, D_MODEL), lambda i: (i // tiles_per_seq, 0, 0))
        tab_spec = pl.BlockSpec((tm, LANES), lambda i: (i % tiles_per_seq, 0))
    else:
        ada_spec = row(D_MODEL)
        tab_spec = pl.BlockSpec((tm, LANES), lambda i: (0, 0))
    wa, wb, qg, kvg, wuq, convw = wts
    in_specs = [row(D_MODEL), ada_spec, ada_spec,
                _const_spec(wa.shape), _const_spec(wb.shape), _const_spec(qg.shape), _const_spec(kvg.shape),
                _const_spec(wuq.shape), _const_spec(convw.shape),
                tab_spec, tab_spec, tab_spec, tab_spec]
    out_shape = [jax.ShapeDtypeStruct((ntok, KV_LORA), F32), jax.ShapeDtypeStruct((ntok, QK_ROPE), F32),
                 jax.ShapeDtypeStruct((ntok, CONV_DIM), BF16)]
    out_specs = [row(KV_LORA), row(QK_ROPE), row(CONV_DIM)]
    if prompt:
        wuk, wuv = extra
        in_specs += [_const_spec(wuk.shape), _const_spec(wuv.shape)]
        head_spec = pl.BlockSpec((None, N_HEADS, tm, HEAD_SLAB),
                                 lambda i: (i // tiles_per_seq, 0, i % tiles_per_seq, 0))
        out_shape += [jax.ShapeDtypeStruct((nb * SUBLANES, CONV_DIM), F32),
                      jax.ShapeDtypeStruct((nb, N_HEADS, seq, HEAD_SLAB), BF16),
                      jax.ShapeDtypeStruct((nb, N_HEADS, seq, HEAD_SLAB), BF16),
                      jax.ShapeDtypeStruct((nb, N_HEADS * V_HEAD, seq), BF16)]
        out_specs += [pl.BlockSpec((SUBLANES, CONV_DIM), lambda i: (i // tiles_per_seq, 0)),
                      head_spec, head_spec,
                      pl.BlockSpec((None, N_HEADS * V_HEAD, tm), lambda i: (i // tiles_per_seq, 0, i % tiles_per_seq))]
    else:
        wukt, fix1, fix2 = extra
        in_specs += [_const_spec(wukt.shape), row(CONV_DIM), row(CONV_DIM)]
        nseq = tm // SUBLANES
        out_shape += [jax.ShapeDtypeStruct((ntok, CONV_DIM), F32),
                      jax.ShapeDtypeStruct((nb, N_HEADS, SUBLANES, KV_LORA), F32),
                      jax.ShapeDtypeStruct((nb, N_HEADS, SUBLANES, QK_ROPE), F32)]
        out_specs += [row(CONV_DIM),
                      pl.BlockSpec((nseq, N_HEADS, SUBLANES, KV_LORA), lambda i: (i, 0, 0, 0)),
                      pl.BlockSpec((nseq, N_HEADS, SUBLANES, QK_ROPE), lambda i: (i, 0, 0, 0))]
    return pl.pallas_call(
        functools.partial(_mixer_in_kernel, prompt=prompt, tm=tm, sub=sub, tiles_per_seq=tiles_per_seq),
        grid=grid, in_specs=in_specs, out_specs=out_specs, out_shape=out_shape,
        scratch_shapes=[pltpu.VMEM((SUBLANES + tm, CONV_DIM), F32)],
        compiler_params=_params(),
        name="mixer_in_prompt" if prompt else "mixer_in_sample",
    )(x2d, sh, sc, wa, wb, qg, kvg, wuq, convw, *tabs, *extra)


def _prompt_attn_kernel(q_ref, k_ref, vt_ref, o_ref, *, seq, tq, heads):
    th = tq // 2
    key_le_query = (lax.broadcasted_iota(jnp.int32, (th, th), 0) <= lax.broadcasted_iota(jnp.int32, (th, th), 1))

    def scores(i, hh):
        lo, mid, hi = i * tq, i * tq + th, (i + 1) * tq
        q = q_ref[hh, lo:hi, :]
        s_aa = lax.dot_general(k_ref[hh, lo:mid, :], q[0:th], NT_DIMS, preferred_element_type=F32)
        s_b = lax.dot_general(k_ref[hh, lo:hi, :], q[th:tq], NT_DIMS, preferred_element_type=F32)
        s_aa = jnp.where(key_le_query, s_aa, -jnp.inf)
        s_bb = jnp.where(key_le_query, s_b[th:tq], -jnp.inf)
        so = lax.dot_general(k_ref[hh, 0:lo, :], q, NT_DIMS, preferred_element_type=F32) if i > 0 else None
        return s_aa, s_b[0:th], s_bb, so

    def softmax_pv(i, hh, s_aa, s_ab, s_bb, so):
        lo, mid, hi = i * tq, i * tq + th, (i + 1) * tq
        vt = vt_ref.at[hh * V_HEAD:(hh + 1) * V_HEAD, :]
        m_a = jnp.max(s_aa, axis=0, keepdims=True)
        m_b = jnp.maximum(jnp.max(s_ab, axis=0, keepdims=True), jnp.max(s_bb, axis=0, keepdims=True))
        m = jnp.concatenate([m_a, m_b], axis=1)
        if so is not None:
            m = jnp.maximum(m, jnp.max(so, axis=0, keepdims=True))
        m_a, m_b = m[:, 0:th], m[:, th:tq]
        p_aa = jnp.exp2(s_aa - m_a)
        p_ab = jnp.exp2(s_ab - m_b)
        p_bb = jnp.exp2(s_bb - m_b)
        l = jnp.concatenate([jnp.sum(p_aa, axis=0, keepdims=True),
                             jnp.sum(p_ab, axis=0, keepdims=True) + jnp.sum(p_bb, axis=0, keepdims=True)], axis=1)
        p_b = jnp.concatenate([p_ab, p_bb], axis=0).astype(BF16)
        acc = jnp.concatenate([_mm(vt[:, lo:mid], p_aa.astype(BF16)), _mm(vt[:, lo:hi], p_b)], axis=1)
        if so is not None:
            po = jnp.exp2(so - m)
            l = l + jnp.sum(po, axis=0, keepdims=True)
            acc = acc + _mm(vt[:, 0:lo], po.astype(BF16))
        return acc * (1.0 / l)

    n_tiles = seq // tq
    units = [(i, hh) for i in reversed(range(n_tiles)) for hh in range(heads)]
    pending = scores(*units[0])
    outs = []
    for n, (i, hh) in enumerate(units):
        nxt = scores(*units[n + 1]) if n + 1 < len(units) else None
        outs.append(softmax_pv(i, hh, *pending))
        pending = nxt
        if hh % 2 == 1:
            o_ref[i * tq:(i + 1) * tq, (hh // 2) * LANES:(hh // 2 + 1) * LANES] = (
                jnp.concatenate(outs, axis=0).T.astype(BF16))
            outs = []


def _prompt_attention(q, k, vt, *, nb, seq, tq=ATTN_TILE, heads=ATTN_HEADS):
    head_spec = pl.BlockSpec((None, heads, seq, HEAD_SLAB), lambda b, p: (b, p, 0, 0))
    return pl.pallas_call(
        functools.partial(_prompt_attn_kernel, seq=seq, tq=tq, heads=heads),
        grid=(nb, N_HEADS // heads),
        in_specs=[head_spec, head_spec, pl.BlockSpec((None, heads * V_HEAD, seq), lambda b, p: (b, p, 0))],
        out_specs=pl.BlockSpec((None, seq, heads * V_HEAD), lambda b, p: (b, 0, p)),
        out_shape=jax.ShapeDtypeStruct((nb, seq, N_HEADS * V_HEAD), BF16),
        compiler_params=_params(2),
        name="prompt_attention",
    )(q, k, vt)


def _decode_attn_kernel(pt_ref, qabs_ref, qr_ref, cnew_ref, krnew_ref, lat_hbm, ropet_hbm, olat_ref, *bufs,
                        n_pages, group, page, n_slots, n_steps, lanes):
    b = pl.program_id(0)
    n_groups = n_pages // group
    gk = group * page
    rows = N_HEADS * SUBLANES
    per_lane = 6 + 2 * n_groups
    lane_bufs = [bufs[j * per_lane:(j + 1) * per_lane] for j in range(lanes)]

    def page_copies(j, step, g):
        latbuf, ropebuf, sems = lane_bufs[j][:3]
        slot = g % n_slots
        out = []
        for p in range(group):
            pid = pt_ref[step * lanes + j, g * group + p]
            out.append(pltpu.make_async_copy(lat_hbm.at[pid], latbuf.at[slot, pl.ds(p * page, page)], sems.at[0, slot]))
            out.append(pltpu.make_async_copy(ropet_hbm.at[pid], ropebuf.at[slot, :, pl.ds(p * page, page)], sems.at[1, slot]))
        return out

    def start_group(j, step, g):
        for n, cp in enumerate(page_copies(j, step, g)):
            cp.start(priority=n % 2)

    @pl.when(b == 0)
    def _():
        for g in range(n_slots):
            for j in range(lanes):
                start_group(j, 0, g)

    def new_token_block(ref, j, width):
        return jnp.concatenate([ref[j], jnp.zeros((SUBLANES, width), F32)], axis=0).astype(BF16)

    def arrive(j, g):
        for cp in page_copies(j, b, g):
            cp.wait()

    def stage(j, g):
        latbuf, ropebuf = lane_bufs[j][:2]
        cbf, krbf = lane_bufs[j][5:5 + n_groups], lane_bufs[j][5 + n_groups:5 + 2 * n_groups]
        cbf[g][...] = latbuf[g % n_slots].astype(BF16)
        krbf[g][...] = ropebuf[g % n_slots].astype(BF16)

    def refill(j, g):
        nxt = g + n_slots
        if nxt < n_groups:
            start_group(j, b, nxt)
        else:
            @pl.when(b + 1 < n_steps)
            def _():
                start_group(j, b + 1, nxt - n_groups)

    def phase1():
        qa = [qabs_ref[j].astype(BF16) for j in range(lanes)]
        qr = [qr_ref[j].astype(BF16) for j in range(lanes)]
        t_row = lax.broadcasted_iota(jnp.int32, (rows, 2 * SUBLANES), 0) % SUBLANES
        k_col = lax.broadcasted_iota(jnp.int32, (rows, 2 * SUBLANES), 1)
        for j in range(lanes):
            s_new = (lax.dot_general(qa[j], new_token_block(cnew_ref, j, KV_LORA), NT_DIMS, preferred_element_type=F32)
                     + lax.dot_general(qr[j], new_token_block(krnew_ref, j, QK_ROPE), NT_DIMS,
                                       preferred_element_type=F32))
            lane_bufs[j][4][...] = jnp.where(k_col <= t_row, s_new, -jnp.inf)
            arrive(j, 0)
        for j in range(lanes):
            stage(j, 0)
        m_lanes = [None] * lanes
        for g in range(n_groups):
            for j in range(lanes):
                refill(j, g)
            if g + 1 < n_groups:
                for j in range(lanes):
                    arrive(j, g + 1)
            for j in range(lanes):
                s_all = lane_bufs[j][3]
                cbf, krbf = lane_bufs[j][5:5 + n_groups], lane_bufs[j][5 + n_groups:5 + 2 * n_groups]
                s = lax.dot_general(qa[j], cbf[g][...], NT_DIMS, preferred_element_type=F32) + _mm(qr[j], krbf[g][...])
                s_all[:, g * gk:(g + 1) * gk] = s
                for k in range(gk // LANES):
                    blk = s[:, k * LANES:(k + 1) * LANES]
                    m_lanes[j] = blk if m_lanes[j] is None else jnp.maximum(m_lanes[j], blk)
                if g + 1 < n_groups:
                    stage(j, g + 1)
        for j in range(lanes):
            lane_bufs[j][5 + 2 * n_groups][...] = m_lanes[j]

    def phase2():
        m, l_new, l_lanes, acc = [], [], [None] * lanes, []
        for j in range(lanes):
            sn = lane_bufs[j][4][...]
            m_part = lane_bufs[j][5 + 2 * n_groups][...]
            m.append(jnp.maximum(jnp.max(sn, axis=-1, keepdims=True), jnp.max(m_part, axis=-1, keepdims=True)))
            p = jnp.exp(sn - m[j])
            l_new.append(jnp.sum(p, axis=-1, keepdims=True))
            acc.append(_mm(p.astype(BF16), new_token_block(cnew_ref, j, KV_LORA)))
        units = [(g, j) for g in range(n_groups) for j in range(lanes)]
        pending = None
        for n in range(len(units) + 1):
            if n < len(units):
                g, j = units[n]
                p = jnp.exp(lane_bufs[j][3][:, g * gk:(g + 1) * gk] - m[j])
                for k in range(gk // LANES):
                    blk = p[:, k * LANES:(k + 1) * LANES]
                    l_lanes[j] = blk if l_lanes[j] is None else l_lanes[j] + blk
                nxt = (g, j, p.astype(BF16))
            else:
                nxt = None
            if pending is not None:
                g0, j0, p0 = pending
                acc[j0] = acc[j0] + _mm(p0, lane_bufs[j0][5 + g0][...])
            pending = nxt
        for j in range(lanes):
            l = l_new[j] + jnp.sum(l_lanes[j], axis=-1, keepdims=True)
            olat_ref[j] = acc[j] * (1.0 / l)

    pl.when(b >= 0)(phase1)
    pl.when(b < n_steps)(phase2)


def _decode_attention(page_table, qabs, qr, c_new, kr_new, cache_lat, cache_rope_t, *,
                      group=DECODE_GROUP, n_slots=DECODE_SLOTS, lanes=DECODE_LANES):
    nb, n_pages = page_table.shape
    page = cache_lat.shape[1]
    rows = N_HEADS * SUBLANES
    n_groups = n_pages // group
    assert n_pages % group == 0 and n_groups % n_slots == 0 and nb % lanes == 0
    per_step = lambda w, r: pl.BlockSpec((lanes, r, w), lambda b, pt: (b, 0, 0))
    lane_scratch = ([pltpu.VMEM((n_slots, group * page, KV_LORA), F32),
                     pltpu.VMEM((n_slots, QK_ROPE, group * page), F32),
                     pltpu.SemaphoreType.DMA((2, n_slots)),
                     pltpu.VMEM((rows, n_pages * page), F32),
                     pltpu.VMEM((rows, 2 * SUBLANES), F32)]
                    + [pltpu.VMEM((group * page, KV_LORA), BF16)] * n_groups
                    + [pltpu.VMEM((QK_ROPE, group * page), BF16)] * n_groups
                    + [pltpu.VMEM((rows, LANES), F32)])
    grid_spec = pltpu.PrefetchScalarGridSpec(
        num_scalar_prefetch=1,
        grid=(nb // lanes,),
        in_specs=[per_step(KV_LORA, rows), per_step(QK_ROPE, rows), per_step(KV_LORA, SUBLANES),
                  per_step(QK_ROPE, SUBLANES),
                  pl.BlockSpec(memory_space=pl.ANY), pl.BlockSpec(memory_space=pl.ANY)],
        out_specs=per_step(KV_LORA, rows),
        scratch_shapes=lane_scratch * lanes,
    )
    return pl.pallas_call(
        functools.partial(_decode_attn_kernel, n_pages=n_pages, group=group, page=page, n_slots=n_slots,
                          n_steps=nb // lanes, lanes=lanes),
        grid_spec=grid_spec,
        out_shape=jax.ShapeDtypeStruct((nb, rows, KV_LORA), F32),
        compiler_params=_params(),
        name="decode_attention",
    )(page_table, qabs, qr, c_new, kr_new, cache_lat, cache_rope_t)


def _uv_kernel(olat_ref, wuv_ref, o_ref):
    n = olat_ref.shape[0] * SUBLANES
    for hd in range(N_HEADS):
        x = olat_ref[:, hd].reshape(n, KV_LORA).astype(BF16)
        o_ref[:, hd * V_HEAD:(hd + 1) * V_HEAD] = _mm(x, wuv_ref[hd]).astype(BF16)


def _uv_project(olat, wuv_heads):
    nb = olat.shape[0]
    return pl.pallas_call(
        _uv_kernel,
        grid=(1,),
        in_specs=[pl.BlockSpec(olat.shape, lambda i: (0, 0, 0, 0)),
                  pl.BlockSpec(wuv_heads.shape, lambda i: (0, 0, 0))],
        out_specs=pl.BlockSpec((nb * SUBLANES, N_HEADS * V_HEAD), lambda i: (0, 0)),
        out_shape=jax.ShapeDtypeStruct((nb * SUBLANES, N_HEADS * V_HEAD), BF16),
        compiler_params=_params(),
        name="uv_project",
    )(olat, wuv_heads)


def _mixer_out_kernel(x_ref, sh_ref, sc_ref, g1_ref, o_ref, cb_ref, wg_ref, woa_ref, woc_ref, wo_ref,
                      lng_ref, lnb_ref, x1_ref, *, tm, sub):
    def front(r):
        h = (x_ref[r, :] * (1.0 + _rows(sc_ref, r, tm)) + _rows(sh_ref, r, tm)).astype(BF16)
        return _mm(h, wg_ref[...]), _mm(o_ref[r, :], woa_ref[...]), _mm(cb_ref[r, :], woc_ref[...])

    def back(r, gates, y_a, y_c):
        mixed = _sigmoid(gates[:, :D_MODEL]) * y_a + _sigmoid(gates[:, D_MODEL:]) * y_c
        mix = _mm(mixed.astype(BF16), wo_ref[...])
        x1_ref[r, :] = _layer_norm(ALPHA * x_ref[r, :] + _rows(g1_ref, r, tm) * mix, lng_ref[...], lnb_ref[...])

    _pipelined(tm, sub, front, back)


def _mixer_out(x2d, sh, sc, g1, o, cb, wts, *, prompt, tm, sub, tiles_per_seq):
    ntok = x2d.shape[0]
    row = lambda w: pl.BlockSpec((tm, w), lambda i: (i, 0))
    ada_spec = (pl.BlockSpec((None, 1, D_MODEL), lambda i: (i // tiles_per_seq, 0, 0)) if prompt else row(D_MODEL))
    return pl.pallas_call(
        functools.partial(_mixer_out_kernel, tm=tm, sub=sub),
        grid=(ntok // tm,),
        in_specs=[row(D_MODEL), ada_spec, ada_spec, ada_spec, row(N_HEADS * V_HEAD), row(CONV_DIM)]
                 + [_const_spec(w.shape) for w in wts],
        out_specs=row(D_MODEL),
        out_shape=jax.ShapeDtypeStruct((ntok, D_MODEL), F32),
        compiler_params=_params(),
        name="mixer_out_prompt" if prompt else "mixer_out_sample",
    )(x2d, sh, sc, g1, o, cb, *wts)


def _ffn_kernel(x1_ref, sh_ref, sc_ref, g2_ref, w1_ref, w3_ref, w2_ref, lng_ref, lnb_ref, y_ref, *, tm, sub):
    def front(r):
        h2 = (x1_ref[r, :] * (1.0 + _rows(sc_ref, r, tm)) + _rows(sh_ref, r, tm)).astype(BF16)
        return _mm(h2, w1_ref[...]), _mm(h2, w3_ref[...])

    def back(r, a, g):
        gated = (a * _sigmoid(a)) * g
        f = _mm(gated.astype(BF16), w2_ref[...])
        y_ref[r, :] = _layer_norm(ALPHA * x1_ref[r, :] + _rows(g2_ref, r, tm) * f, lng_ref[...], lnb_ref[...])

    _pipelined(tm, sub, front, back)


def _ffn(x1, sh, sc, g2, wts, *, prompt, tm, sub, tiles_per_seq):
    ntok = x1.shape[0]
    row = lambda w: pl.BlockSpec((tm, w), lambda i: (i, 0))
    ada_spec = (pl.BlockSpec((None, 1, D_MODEL), lambda i: (i // tiles_per_seq, 0, 0)) if prompt else row(D_MODEL))
    return pl.pallas_call(
        functools.partial(_ffn_kernel, tm=tm, sub=sub),
        grid=(ntok // tm,),
        in_specs=[row(D_MODEL), ada_spec, ada_spec, ada_spec] + [_const_spec(w.shape) for w in wts],
        out_specs=row(D_MODEL),
        out_shape=jax.ShapeDtypeStruct((ntok, D_MODEL), F32),
        compiler_params=_params(),
        name="ffn_prompt" if prompt else "ffn_sample",
    )(x1, sh, sc, g2, *wts)


def _rope_tables(pos, scale_q):
    inv = ROPE_THETA ** (-jnp.arange(0, QK_ROPE, 2, dtype=F32) / QK_ROPE)
    ang = pos.astype(F32)[:, None] * inv[None, :]
    cos, sin = jnp.cos(ang), jnp.sin(ang)
    n = pos.shape[0]
    z_lo = jnp.zeros((n, ROPE_LO), F32)
    z_hi = jnp.zeros((n, LANES - ROPE_LO - QK_ROPE), F32)
    cos_k = jnp.concatenate([z_lo, cos, cos, z_hi], axis=1)
    sin_k = jnp.concatenate([z_lo, -sin, sin, z_hi], axis=1)
    cos_q = jnp.concatenate([jnp.ones((n, ROPE_LO), F32), cos, cos, z_hi], axis=1) * scale_q
    sin_q = sin_k * scale_q
    return cos_q, sin_q, cos_k, sin_k


def _head_groups(w, width):
    k = w.shape[0]
    return jnp.pad(w, ((0, 0), (0, 0), (0, HEAD_SLAB - width))).reshape(k, N_HEADS * HEAD_SLAB)


def kernel(x_prompt, x_sample, cache_kv_latent, cache_k_rope, state_conv, page_table, c_prompt, c_sample,
           w_ada, b_ada, w_in, q_norm_g, kv_norm_g, w_uq, w_ukv, w_oa, conv_w, w_oc, w_o,
           ln1_g, ln1_b, w_ff1, w_ff3, w_ff2, ln2_g, ln2_b):
    assert w_ada.shape[0] == DEPTH == 1
    nb_p, seq_p, _ = x_prompt.shape
    nb_s, seq_s, _ = x_sample.shape
    assert seq_s == SUBLANES
    n_pool, page = cache_kv_latent.shape[1], cache_kv_latent.shape[2]
    past_len = page_table.shape[1] * page
    layer = 0

    s0, s1, s2, s3 = Q_LORA, Q_LORA + KV_LORA, Q_LORA + KV_LORA + QK_ROPE, Q_LORA + KV_LORA + QK_ROPE + 3 * CONV_DIM
    w_in_l = w_in[layer]
    w_kr_group = jnp.pad(w_in_l[:, s1:s2], ((0, 0), (ROPE_LO, LANES - ROPE_LO - QK_ROPE)))
    wa = jnp.concatenate([w_in_l[:, :s1], w_kr_group], axis=1).astype(BF16)
    wb = w_in_l[:, s2:s3].astype(BF16)
    wg = w_in_l[:, s3:].astype(BF16)
    wuq = _head_groups(w_uq[layer].reshape(Q_LORA, N_HEADS, QK_HEAD), QK_HEAD).astype(BF16)
    w_uk = w_ukv[layer][:, :, :QK_NOPE]
    w_uv = w_ukv[layer][:, :, QK_NOPE:]
    wuk_groups = _head_groups(w_uk, QK_NOPE).astype(BF16)
    wuv_t = w_uv.reshape(KV_LORA, N_HEADS * V_HEAD).T.astype(BF16)
    wuv_heads = w_uv.transpose(1, 0, 2).astype(BF16)
    wukt = jnp.pad(w_uk.transpose(1, 2, 0), ((0, 0), (0, HEAD_SLAB - QK_NOPE), (0, 0))).astype(BF16)
    qg = q_norm_g[layer][None, :]
    kvg = kv_norm_g[layer][None, :]
    convw = conv_w[layer]
    in_wts = (wa, wb, qg, kvg, wuq, convw)
    out_wts = (wg, w_oa[layer].astype(BF16), w_oc[layer].astype(BF16), w_o[layer].astype(BF16),
               ln1_g[layer][None, :], ln1_b[layer][None, :])
    ffn_wts = (w_ff1[layer].astype(BF16), w_ff3[layer].astype(BF16), w_ff2[layer].astype(BF16),
               ln2_g[layer][None, :], ln2_b[layer][None, :])

    ada = _ada_terms(jnp.concatenate([c_prompt, c_sample], axis=0), w_ada[layer].astype(BF16), b_ada[layer][None, :])
    ada_p = [a[:, None, :] for a in jnp.split(ada[:nb_p], 6, axis=-1)]
    ada_s = [jnp.repeat(a, seq_s, axis=0) for a in jnp.split(ada[nb_p:], 6, axis=-1)]

    tm_p = PROMPT_TILE
    assert seq_p % tm_p == 0 and tm_p % SUB_TILE == 0 and seq_p % ATTN_TILE == 0
    tps = seq_p // tm_p
    xp = x_prompt.reshape(nb_p * seq_p, D_MODEL)
    tabs_p = _rope_tables(jnp.arange(seq_p), ATTN_SCALE * math.log2(math.e))
    ckv_p, kr_p, cb_p, u_p, q_p, k_p, vt_p = _mixer_in(
        xp, ada_p[0], ada_p[1], in_wts, tabs_p, prompt=True, nb=nb_p, seq=seq_p, tm=tm_p, sub=SUB_TILE,
        extra=(wuk_groups, wuv_t))
    o_p = _prompt_attention(q_p, k_p, vt_p, nb=nb_p, seq=seq_p)
    x1_p = _mixer_out(xp, ada_p[0], ada_p[1], ada_p[2], o_p.reshape(nb_p * seq_p, N_HEADS * V_HEAD), cb_p, out_wts,
                      prompt=True, tm=tm_p, sub=SUB_TILE, tiles_per_seq=tps)
    y_p = _ffn(x1_p, ada_p[3], ada_p[4], ada_p[5], ffn_wts, prompt=True, tm=tm_p, sub=FFN_SUB_TILE, tiles_per_seq=tps)

    tm_s = SAMPLE_TILE
    ntok_s = nb_s * seq_s
    assert ntok_s % tm_s == 0 and tm_s % seq_s == 0
    xs = x_sample.reshape(ntok_s, D_MODEL)
    tabs_s = tuple(jnp.tile(t, (tm_s // seq_s, 1)) for t in _rope_tables(past_len + jnp.arange(seq_s), ATTN_SCALE))
    st = state_conv[layer]
    zrow = jnp.zeros((nb_s, seq_s - 1, CONV_DIM), F32)
    fix1 = jnp.concatenate([st[:, 1:2], zrow], axis=1).reshape(ntok_s, CONV_DIM)
    fix2 = jnp.concatenate([st, zrow[:, 1:]], axis=1).reshape(ntok_s, CONV_DIM)
    ckv_s, kr_s, cb_s, u_s, qabs_s, qr_s = _mixer_in(
        xs, ada_s[0], ada_s[1], in_wts, tabs_s, prompt=False, nb=nb_s, seq=seq_s, tm=tm_s, sub=tm_s,
        extra=(wukt, fix1, fix2))
    rows = N_HEADS * seq_s
    olat = _decode_attention(
        page_table, qabs_s.reshape(nb_s, rows, KV_LORA), qr_s.reshape(nb_s, rows, QK_ROPE),
        ckv_s.reshape(nb_s, seq_s, KV_LORA), kr_s.reshape(nb_s, seq_s, QK_ROPE),
        cache_kv_latent.reshape(n_pool, page, KV_LORA),
        jnp.swapaxes(cache_k_rope.reshape(n_pool, page, QK_ROPE), 1, 2))
    o_s = _uv_project(olat.reshape(nb_s, N_HEADS, seq_s, KV_LORA), wuv_heads)
    x1_s = _mixer_out(xs, ada_s[0], ada_s[1], ada_s[2], o_s, cb_s, out_wts, prompt=False, tm=tm_s, sub=tm_s, tiles_per_seq=1)
    y_s = _ffn(x1_s, ada_s[3], ada_s[4], ada_s[5], ffn_wts, prompt=False, tm=tm_s, sub=tm_s, tiles_per_seq=1)

    hist = CONV_WIDTH - 1
    return (y_p.reshape(nb_p, seq_p, D_MODEL),
            y_s.reshape(nb_s, seq_s, D_MODEL),
            ckv_p.reshape(1, nb_p, seq_p, KV_LORA),
            kr_p.reshape(1, nb_p, seq_p, QK_ROPE),
            u_p.reshape(1, nb_p, SUBLANES, CONV_DIM)[:, :, SUBLANES - hist:],
            ckv_s.reshape(1, nb_s, seq_s, KV_LORA),
            kr_s.reshape(1, nb_s, seq_s, QK_ROPE),
            u_s.reshape(1, nb_s, seq_s, CONV_DIM)[:, :, seq_s - hist:])
```
